```python
import math
import jax
import jax.numpy as jnp
from jax import lax
import numpy as np

D_MODEL = 1024
BATCH = 4
SEQ = 4096
DEPTH = 4
DEC_BATCH = 32
DEC_SEQ = 8
PAST_LEN = 8192
PAGE_SIZE = 128

HEAD_DIM = D_MODEL // 16
HA = 4
DKA = HEAD_DIM
DVA = 2 * HEAD_DIM
RET_CHUNK = 128
RET_THETA = 10000.0
HB = 4
DH = HEAD_DIM
FORGET_BIAS_INIT = 3.0
HC = 4
DC = HEAD_DIM // 2
HD = 4
HI = 4
DI = HEAD_DIM
TOPK_MAX = 256
ROPE_THETA = 500000.0
ROPE_FRAC = 4
Q_BLOCK = 128
D_FF = -(-8 * D_MODEL // (3 * 256)) * 256
N_BRANCH = 4
EPS = 1e-6
SPLITS = (HA * DKA, HA * DKA, HA * DVA, HA * DVA,
          HB * DH, HB * DH, HB * DH, HB,
          HC * 2 * DC, HC * 2 * DC, HC * 2 * DC,
          HD * DH, HD * DH, HD * DH, HI * DI, DI, HI)
N_IN = sum(SPLITS)

kernel_name = 'hybrid_ret_fox_diff_dsa_decode_step'


def rms_norm(x, g=None):
    xf = x.astype(jnp.float32)
    y = xf * lax.rsqrt(jnp.mean(xf * xf, axis=-1, keepdims=True) + EPS)
    if g is not None:
        y = y * g.astype(jnp.float32)
    return y.astype(x.dtype)


def rope(x, pos, rot, theta):
    half = rot // 2
    inv = theta ** (-jnp.arange(half, dtype=jnp.float32) / half)
    ang = pos.astype(jnp.float32)[:, None] * inv[None, :]
    cos = jnp.cos(ang)[None, :, None, :]
    sin = jnp.sin(ang)[None, :, None, :]
    xf = x.astype(jnp.float32)
    x1, x2 = xf[..., :half], xf[..., half:rot]
    out = jnp.concatenate([x1 * cos - x2 * sin, x2 * cos + x1 * sin, xf[..., rot:]], axis=-1)
    return out.astype(x.dtype)


def to_blocks(a, n):
    b, t = a.shape[:2]
    return jnp.moveaxis(a.reshape(b, t // n, n, *a.shape[2:]), 1, 0)


def from_blocks(a):
    a = jnp.moveaxis(a, 0, 1)
    return a.reshape(a.shape[0], a.shape[1] * a.shape[2], *a.shape[3:])


def over_query_blocks(fn, *q_side):
    t = q_side[0].shape[1]
    qpos = jnp.arange(t, dtype=jnp.int32).reshape(t // Q_BLOCK, Q_BLOCK)
    out = lax.map(lambda args: fn(*args), (qpos,) + tuple(to_blocks(a, Q_BLOCK) for a in q_side))
    return from_blocks(out)


def take_rows(a, idx):
    return jax.vmap(lambda r, i: r[i])(a, idx)


def gather_pages(pool, page_table):
    g = pool[page_table]
    return g.reshape(g.shape[0], g.shape[1] * g.shape[2], *g.shape[3:])


def project(h, w_in, b_forget, pos):
    b, t = h.shape[:2]
    parts = jnp.split(h @ w_in, np.cumsum(SPLITS)[:-1].tolist(), axis=-1)
    (qa, ka, va, ga, qb, kb, vb, fb, qc, kc, vc, qd, kd, vd, qi, ki, wi) = parts
    heads = lambda a, n, d: a.reshape(b, t, n, d)
    part_rope = lambda a, d: rope(a, pos, d // ROPE_FRAC, ROPE_THETA)
    return dict(
        qa=rope(heads(qa, HA, DKA), pos, DKA, RET_THETA),
        ka=rope(heads(ka, HA, DKA), pos, DKA, RET_THETA) * DKA ** -0.5,
        va=heads(va, HA, DVA),
        ga=ga,
        qb=heads(qb, HB, DH),
        kb=heads(kb, HB, DH),
        vb=heads(vb, HB, DH),
        lf=jax.nn.log_sigmoid((fb + b_forget).astype(jnp.float32)),
        qc=part_rope(heads(qc, 2 * HC, DC), DC).reshape(b, t, HC, 2, DC),
        kc=part_rope(heads(kc, 2 * HC, DC), DC).reshape(b, t, HC, 2, DC),
        vc=heads(vc, HC, 2 * DC),
        qd=part_rope(heads(qd, HD, DH), DH),
        kd=part_rope(heads(kd, HD, DH), DH),
        vd=heads(vd, HD, DH),
        qi=part_rope(heads(qi, HI, DI), DI),
        ki=part_rope(ki[:, :, None, :], DI)[:, :, 0],
        wi=wi)


def retention_chunk(r, q, k, v, log_gamma):
    c = q.shape[1]
    i = jnp.arange(c, dtype=jnp.float32)
    rel = i[:, None] - i[None, :]
    dmask = jnp.where(rel[None] >= 0, jnp.exp(jnp.maximum(rel, 0.0)[None] * log_gamma[:, None, None]), 0.0)
    qf, kf, vf = q.astype(jnp.float32), k.astype(jnp.float32), v.astype(jnp.float32)
    s = jnp.einsum('bihd,bjhd->bhij', qf, kf) * dmask[None]
    o = jnp.einsum('bhij,bjhe->bihe', s, vf)
    xi = jnp.exp((i[:, None] + 1.0) * log_gamma[None, :])
    o = o + jnp.einsum('bihd,bhde->bihe', qf * xi[None, :, :, None], r)
    zeta = jnp.exp((c - 1.0 - i)[:, None] * log_gamma[None, :])
    r_new = (jnp.exp(c * log_gamma)[None, :, None, None] * r
             + jnp.einsum('bjhd,bjhe->bhde', kf * zeta[None, :, :, None], vf))
    return r_new, o


def retention_out(o, ga):
    b, t = ga.shape[:2]
    return rms_norm(o).reshape(b, t, HA * DVA).astype(ga.dtype) * jax.nn.silu(ga)


def fox_core(q, cq, qpos, k, v, ck, kpos):
    s = jnp.einsum('bqhd,bkhd->bhqk', q, k).astype(jnp.float32) * DH ** -0.5
    s = s + jnp.moveaxis(cq, 1, 2)[..., None] - jnp.moveaxis(ck, 1, 2)[:, :, None, :]
    mask = kpos[None, :] <= qpos[:, None]
    p = jax.nn.softmax(jnp.where(mask, s, -jnp.inf), axis=-1)
    return jnp.einsum('bhqk,bkhd->bqhd', p.astype(v.dtype), v)


def diff_core(q, qpos, k, v, kpos, lam):
    s = jnp.einsum('bqhmd,bkhmd->bhmqk', q, k).astype(jnp.float32) * DC ** -0.5
    mask = kpos[None, :] <= qpos[:, None]
    p = jax.nn.softmax(jnp.where(mask, s, -jnp.inf), axis=-1)
    a = p[:, :, 0] - lam * p[:, :, 1]
    return jnp.einsum('bhqk,bkhe->bqhe', a.astype(v.dtype), v)


def diff_out(o, g, lam_init):
    b, t = o.shape[:2]
    return (rms_norm(o, g.reshape(HC, 2 * DC)) * (1.0 - lam_init)).reshape(b, t, HC * 2 * DC)


def dsa_select(qi, wi, qpos, ik, kpos, n_sel):
    logits = jnp.einsum('bqhd,bkd->bqhk', qi, ik).astype(jnp.float32) * DI ** -0.5
    score = jnp.einsum('bqhk,bqh->bqk', jax.nn.relu(logits), wi.astype(jnp.float32)) * HI ** -0.5
    score = jnp.where(kpos[None, None, :] <= qpos[None, :, None], score, -jnp.inf)
    return lax.top_k(score, n_sel)[1]


def dsa_attend(q, qpos, k_sel, v_sel, sel_pos):
    s = jnp.einsum('bqhd,bqkhd->bhqk', q, k_sel).astype(jnp.float32) * DH ** -0.5
    mask = (sel_pos <= qpos[None, :, None])[:, None]
    p = jax.nn.softmax(jnp.where(mask, s, -jnp.inf), axis=-1)
    return jnp.einsum('bhqk,bqkhd->bqhd', p.astype(v_sel.dtype), v_sel)


def gather_selected(pool, new, page_table, idx, past):
    s = new.shape[1]
    idx_p = jnp.minimum(idx, past - 1)
    phys_page = jax.vmap(lambda pt, i: pt[i])(page_table, idx_p // PAGE_SIZE)
    rows = phys_page * PAGE_SIZE + idx_p % PAGE_SIZE
    old = pool.reshape(-1, *pool.shape[2:])[rows].astype(new.dtype)
    fresh = take_rows(new, jnp.clip(idx - past, 0, s - 1))
    return jnp.where((idx < past)[..., None, None], old, fresh)


def prompt_mixers(pr, lam, log_gamma):
    b, t = pr['ga'].shape[:2]
    r0 = jnp.zeros((b, HA, DKA, DVA), jnp.float32)
    step = lambda r, xs: retention_chunk(r, xs[0], xs[1], xs[2], log_gamma)
    r_fin, o_a = lax.scan(step, r0, tuple(to_blocks(pr[n], RET_CHUNK) for n in ('qa', 'ka', 'va')))
    o_a = from_blocks(o_a)
    kpos = jnp.arange(t, dtype=jnp.int32)
    c = jnp.cumsum(pr['lf'], axis=1)
    o_b = over_query_blocks(lambda qp, q, cq: fox_core(q, cq, qp, pr['kb'], pr['vb'], c, kpos), pr['qb'], c)
    o_c = over_query_blocks(lambda qp, q: diff_core(q, qp, pr['kc'], pr['vc'], kpos, lam), pr['qc'])
    n_sel = min(TOPK_MAX, t // 4)

    def dsa_block(qp, q, qi, wi):
        idx = dsa_select(qi, wi, qp, pr['ki'], kpos, n_sel)
        return dsa_attend(q, qp, take_rows(pr['kd'], idx), take_rows(pr['vd'], idx), idx)

    o_d = over_query_blocks(dsa_block, pr['qd'], pr['qi'], pr['wi'])
    return o_a, o_b, o_c, o_d, r_fin


def sample_mixers(sm, r_prev, fk, fv, flf, ck, cv, sk, sv, sik, page_table, lam, log_gamma):
    b, s = sm['ga'].shape[:2]
    past = page_table.shape[1] * PAGE_SIZE
    qpos = past + jnp.arange(s, dtype=jnp.int32)
    kpos = jnp.arange(past + s, dtype=jnp.int32)
    cat = lambda pool, new: jnp.concatenate([gather_pages(pool, page_table).astype(new.dtype), new], axis=1)
    r_new, o_a = retention_chunk(r_prev.astype(jnp.float32), sm['qa'], sm['ka'], sm['va'], log_gamma)
    c = jnp.cumsum(cat(flf, sm['lf']).astype(jnp.float32), axis=1)
    o_b = fox_core(sm['qb'], c[:, past:], qpos, cat(fk, sm['kb']), cat(fv, sm['vb']), c, kpos)
    kc_all = cat(ck, sm['kc'].reshape(b, s, HC, 2 * DC)).reshape(b, past + s, HC, 2, DC)
    o_c = diff_core(sm['qc'], qpos, kc_all, cat(cv, sm['vc']), kpos, lam)
    n_sel = min(TOPK_MAX, (past + s) // 4)
    idx = dsa_select(sm['qi'], sm['wi'], qpos, cat(sik, sm['ki']), kpos, n_sel)
    k_sel = gather_selected(sk, sm['kd'], page_table, idx, past)
    v_sel = gather_selected(sv, sm['vd'], page_table, idx, past)
    o_d = dsa_attend(sm['qd'], qpos, k_sel, v_sel, idx)
    return o_a, o_b, o_c, o_d, r_new


def swiglu(h, w_gu, w_down):
    g, u = jnp.split(h @ w_gu, 2, axis=-1)
    return (jax.nn.silu(g) * u) @ w_down


def finish_layer(x, h, ga, o_a, o_b, o_c, o_d, lam_init, g_diff, w_pa, w_pb, w_pc, w_pd,
                 w_gate, b_gate, w_out, g_post_mix, g_pre_ffn, g_post_ffn, w_gu, w_down):
    b, t = x.shape[:2]
    br_a = retention_out(o_a, ga) @ w_pa
    br_b = o_b.reshape(b, t, HB * DH) @ w_pb
    br_c = diff_out(o_c, g_diff, lam_init) @ w_pc
    br_d = o_d.reshape(b, t, HD * DH) @ w_pd
    gates = jax.nn.sigmoid((h @ w_gate + b_gate).astype(jnp.float32)).astype(x.dtype)
    gates = gates.reshape(b, t, N_BRANCH, D_MODEL)
    mixed = (gates[:, :, 0] * br_a + gates[:, :, 1] * br_b
             + gates[:, :, 2] * br_c + gates[:, :, 3] * br_d)
    x = x + rms_norm(mixed @ w_out, g_post_mix)
    return x + rms_norm(swiglu(rms_norm(x, g_pre_ffn), w_gu, w_down), g_post_ffn)


def setup_inputs(seed: int = 0) -> dict:
    key = jax.random.key(seed)
    keys = jax.random.split(key, 40)
    counter = [0]

    def nrm(shape, scale=1.0):
        k = keys[counter[0]]
        counter[0] += 1
        return jax.random.normal(k, shape, jnp.float32) * scale

    n_pages = PAST_LEN // PAGE_SIZE
    used = DEC_BATCH * n_pages
    n_pool = (used * 5) // 4
    gain = lambda shape: 1.0 + nrm(shape, 0.05)
    inp = {}
    inp['x_prompt'] = nrm((BATCH, SEQ, D_MODEL))
    inp['x_sample'] = nrm((DEC_BATCH, DEC_SEQ, D_MODEL))
    inp['state_ret'] = nrm((DEPTH, DEC_BATCH, HA, DKA, DVA))
    inp['cache_fox_k'] = nrm((DEPTH, n_pool, PAGE_SIZE, HB, DH))
    inp['cache_fox_v'] = nrm((DEPTH, n_pool, PAGE_SIZE, HB, DH))
    inp['cache_fox_lf'] = jax.nn.log_sigmoid(FORGET_BIAS_INIT + nrm((DEPTH, n_pool, PAGE_SIZE, HB)))
    inp['cache_diff_k'] = nrm((DEPTH, n_pool, PAGE_SIZE, HC, 2 * DC))
    inp['cache_diff_v'] = nrm((DEPTH, n_pool, PAGE_SIZE, HC, 2 * DC))
    inp['cache_dsa_k'] = nrm((DEPTH, n_pool, PAGE_SIZE, HD, DH))
    inp['cache_dsa_v'] = nrm((DEPTH, n_pool, PAGE_SIZE, HD, DH))
    inp['cache_dsa_ik'] = nrm((DEPTH, n_pool, PAGE_SIZE, DI))
    inp['page_table'] = jax.random.permutation(keys[39], n_pool)[:used].reshape(DEC_BATCH, n_pages).astype(jnp.int32)
    inp['g_pre_mix'] = gain((DEPTH, D_MODEL))
    inp['g_post_mix'] = gain((DEPTH, D_MODEL))
    inp['g_pre_ffn'] = gain((DEPTH, D_MODEL))
    inp['g_post_ffn'] = gain((DEPTH, D_MODEL))
    inp['w_in'] = nrm((DEPTH, D_MODEL, N_IN), D_MODEL ** -0.5)
    inp['b_forget'] = FORGET_BIAS_INIT + nrm((DEPTH, HB), 0.1)
    inp['lam_q1'] = nrm((DEPTH, DC), 0.1)
    inp['lam_k1'] = nrm((DEPTH, DC), 0.1)
    inp['lam_q2'] = nrm((DEPTH, DC), 0.1)
    inp['lam_k2'] = nrm((DEPTH, DC), 0.1)
    inp['g_diff'] = gain((DEPTH, HC * 2 * DC))
    inp['w_pa'] = nrm((DEPTH, HA * DVA, D_MODEL), (HA * DVA) ** -0.5)
    inp['w_pb'] = nrm((DEPTH, HB * DH, D_MODEL), (HB * DH) ** -0.5)
    inp['w_pc'] = nrm((DEPTH, HC * 2 * DC, D_MODEL), (HC * 2 * DC) ** -0.5)
    inp['w_pd'] = nrm((DEPTH, HD * DH, D_MODEL), (HD * DH) ** -0.5)
    inp['w_gate'] = nrm((DEPTH, D_MODEL, N_BRANCH * D_MODEL), D_MODEL ** -0.5)
    inp['b_gate'] = nrm((DEPTH, N_BRANCH * D_MODEL), 0.01)
    inp['w_out'] = nrm((DEPTH, D_MODEL, D_MODEL), D_MODEL ** -0.5)
    inp['w_gu'] = nrm((DEPTH, D_MODEL, 2 * D_FF), D_MODEL ** -0.5)
    inp['w_down'] = nrm((DEPTH, D_FF, D_MODEL), D_FF ** -0.5)
    return inp


def reference(x_prompt, x_sample, state_ret, cache_fox_k, cache_fox_v, cache_fox_lf,
              cache_diff_k, cache_diff_v, cache_dsa_k, cache_dsa_v, cache_dsa_ik, page_table,
              g_pre_mix, g_post_mix, g_pre_ffn, g_post_ffn, w_in, b_forget,
              lam_q1, lam_k1, lam_q2, lam_k2, g_diff, w_pa, w_pb, w_pc, w_pd,
              w_gate, b_gate, w_out, w_gu, w_down):
    f32 = jnp.float32
    log_gamma = jnp.log1p(-jnp.exp2(-5.0 - jnp.arange(HA, dtype=f32)))
    bp, tp = x_prompt.shape[:2]
    bs, ts = x_sample.shape[:2]
    past = page_table.shape[1] * PAGE_SIZE
    pos_p = jnp.arange(tp, dtype=jnp.int32)
    pos_s = past + jnp.arange(ts, dtype=jnp.int32)
    names = ('ret_p', 'ret_s', 'fk_p', 'fv_p', 'flf_p', 'fk_s', 'fv_s', 'flf_s',
             'ck_p', 'cv_p', 'ck_s', 'cv_s', 'dk_p', 'dv_p', 'dik_p', 'dk_s', 'dv_s', 'dik_s')
    new = {n: [] for n in names}
    xp, xs = x_prompt, x_sample
    for l in range(DEPTH):
        lam_init = 0.8 - 0.6 * math.exp(-0.3 * l)
        lam = (jnp.exp(jnp.sum(lam_q1[l].astype(f32) * lam_k1[l].astype(f32)))
               - jnp.exp(jnp.sum(lam_q2[l].astype(f32) * lam_k2[l].astype(f32))) + lam_init)
        lw = (lam_init, g_diff[l], w_pa[l], w_pb[l], w_pc[l], w_pd[l], w_gate[l], b_gate[l], w_out[l],
              g_post_mix[l], g_pre_ffn[l], g_post_ffn[l], w_gu[l], w_down[l])
        h = rms_norm(xp, g_pre_mix[l])
        pr = project(h, w_in[l], b_forget[l], pos_p)
        o_a, o_b, o_c, o_d, r_p = prompt_mixers(pr, lam, log_gamma)
        xp = finish_layer(xp, h, pr['ga'], o_a, o_b, o_c, o_d, *lw)
        new['ret_p'].append(r_p)
        new['fk_p'].append(pr['kb'])
        new['fv_p'].append(pr['vb'])
        new['flf_p'].append(pr['lf'])
        new['ck_p'].append(pr['kc'].reshape(bp, tp, HC, 2 * DC))
        new['cv_p'].append(pr['vc'])
        new['dk_p'].append(pr['kd'])
        new['dv_p'].append(pr['vd'])
        new['dik_p'].append(pr['ki'])
        h = rms_norm(xs, g_pre_mix[l])
        sm = project(h, w_in[l], b_forget[l], pos_s)
        o_a, o_b, o_c, o_d, r_s = sample_mixers(
            sm, state_ret[l], cache_fox_k[l], cache_fox_v[l], cache_fox_lf[l], cache_diff_k[l],
            cache_diff_v[l], cache_dsa_k[l], cache_dsa_v[l], cache_dsa_ik[l], page_table, lam, log_gamma)
        xs = finish_layer(xs, h, sm['ga'], o_a, o_b, o_c, o_d, *lw)
        new['ret_s'].append(r_s)
        new['fk_s'].append(sm['kb'])
        new['fv_s'].append(sm['vb'])
        new['flf_s'].append(sm['lf'])
        new['ck_s'].append(sm['kc'].reshape(bs, ts, HC, 2 * DC))
        new['cv_s'].append(sm['vc'])
        new['dk_s'].append(sm['kd'])
        new['dv_s'].append(sm['vd'])
        new['dik_s'].append(sm['ki'])
    st = lambda n: jnp.stack(new[n], axis=0)
    return (xp, xs, st('ret_p'), st('ret_s'),
            st('fk_p'), st('fv_p'), st('flf_p'), st('fk_s'), st('fv_s'), st('flf_s'),
            st('ck_p'), st('cv_p'), st('ck_s'), st('cv_s'),
            st('dk_p'), st('dv_p'), st('dik_p'), st('dk_s'), st('dv_s'), st('dik_s'))
```

```python
import functools
import math

import jax
import jax.numpy as jnp
from jax import lax
from jax.experimental import pallas as pl
from jax.experimental.pallas import tpu as pltpu

F32 = jnp.float32
BF16 = jnp.bfloat16
I32 = jnp.int32

D_MODEL = 1024
HEAD_DIM = 64
N_HEADS = 4
DKA = HEAD_DIM
DVA = 2 * HEAD_DIM
RET_CHUNK = 128
RET_THETA = 10000.0
DH = HEAD_DIM
DC = HEAD_DIM // 2
DI = HEAD_DIM
TOPK_MAX = 256
ROPE_THETA = 500000.0
ROPE_FRAC = 4
D_FF = 2816
EPS = 1e-6
PAGE_SIZE = 128
LANES = 128
NEG = -1e30
INT_MIN = -2 ** 31
VMEM_LIMIT = 56 * 1024 * 1024

LOG_GAMMA = tuple(math.log1p(-(2.0 ** (-5.0 - h))) for h in range(N_HEADS))

_SRC = dict(qa=(0, 256), ka=(256, 256), va=(512, 512), ga=(1024, 512),
            qb=(1536, 256), kb=(1792, 256), vb=(2048, 256), fb=(2304, 4),
            qc=(2308, 256), kc=(2564, 256), vc=(2820, 256),
            qd=(3076, 256), kd=(3332, 256), vd=(3588, 256),
            qi=(3844, 256), ki=(4100, 64), wi=(4164, 4))
_PACK_ORDER = ('qa', 'ka', 'va', 'ga', 'qb', 'kb', 'vb', 'qc', 'kc', 'vc', 'qd', 'kd', 'vd', 'qi')
_GRP = {}
_off = 0
for _n in _PACK_ORDER:
    _GRP[_n] = (_off, _SRC[_n][1])
    _off += _SRC[_n][1]
_GRP['ki'] = (_off, LANES)
_off += LANES
_GRP['aux'] = (_off, LANES)
_off += LANES
N_PACK = _off


def _cparams(sem):
    return pltpu.CompilerParams(dimension_semantics=sem, vmem_limit_bytes=VMEM_LIMIT)


def _rms(x, g=None):
    y = x * lax.rsqrt(jnp.mean(x * x, axis=-1, keepdims=True) + EPS)
    return y if g is None else y * g


def _dot(a, b):
    return jnp.dot(a, b, preferred_element_type=F32)


def _dot_nt(a, b):
    return lax.dot_general(a, b, (((1,), (1,)), ((), ())), preferred_element_type=F32)


def _split3(x):
    x1 = x.astype(BF16)
    r1 = x - x1.astype(F32)
    x2 = r1.astype(BF16)
    x3 = (r1 - x2.astype(F32)).astype(BF16)
    return x1, x2, x3


def _lane_mask(width, lo, hi):
    lane = lax.broadcasted_iota(I32, (1, width), 1)
    return (lane >= lo) & (lane < hi)


def _masked_pair(q, g, w):
    p = (g * w) // LANES
    lo = g * w - p * LANES
    qp = q[:, p * LANES:(p + 1) * LANES]
    return jnp.where(_lane_mask(LANES, lo, lo + w), qp, jnp.zeros_like(qp)), p


def _order_key(score):
    bits = lax.bitcast_convert_type(score + 0.0, I32)
    return jnp.where(bits < 0, bits ^ jnp.int32(0x7FFFFFFF), bits)


def _kth_largest_key(count_ge, k, shape):
    nonneg = count_ge(jnp.zeros(shape, I32)) >= k
    lo0 = jnp.where(nonneg, jnp.int32(0), jnp.int32(INT_MIN))

    def body(b, lo):
        cand = lo + jnp.left_shift(jnp.int32(1), 30 - b)
        return jnp.where(count_ge(cand) >= k, cand, lo)

    return lax.fori_loop(0, 31, body, lo0)


_PROMPT_OUTS = (('qa', 'tok', BF16), ('ka', 'T', BF16), ('va', 'tok', BF16), ('ga', 'tok', F32),
                ('qb', 'tok', BF16), ('kb', 'T', F32), ('vb', 'T', F32),
                ('qc', 'tok', BF16), ('kc', 'T', F32), ('vc', 'T', F32),
                ('qd', 'tok', BF16), ('kd', 'T', F32), ('vd', 'T', F32),
                ('qi', 'tok', BF16), ('ki', 'T', F32), ('comb', 'tok', F32), ('auxT', 'T', F32))
_SAMPLE_OUTS = (('qa', 'tok', F32), ('ka', 'T', F32), ('va', 'tok', F32), ('ga', 'tok', F32),
                ('qb', 'tok', F32), ('kb', 'tok', F32), ('vb', 'tok', F32),
                ('qc', 'tok', F32), ('kc', 'tok', F32), ('vc', 'tok', F32),
                ('qd', 'tok', F32), ('kd', 'tok', F32), ('vd', 'tok', F32),
                ('qi', 'tok', F32), ('ki', 'tok', F32), ('comb', 'tok', F32))
_ROPE_KIND = dict(qa='ret', ka='ret', qc='r32', kc='r32', qd='r64', kd='r64', qi='r64', ki='r64')
_SCALE = dict(ka=DKA ** -0.5, qb=DH ** -0.5, qc=DC ** -0.5, qd=DH ** -0.5)
_T_ROWS = dict(ki=DI, auxT=16)


def _proj_kernel(*refs, outs, with_cumsum, tiles_per_seq):
    (x_ref, g_ref, w_ref, bf_ref, cr_ref, sr_ref, c32_ref, s32_ref, c64_ref, s64_ref) = refs[:10]
    out_refs = dict(zip([o[0] for o in outs], refs[10:10 + len(outs)]))
    tm = x_ref.shape[0]
    h = _rms(x_ref[...], g_ref[...]).astype(BF16)
    tables = dict(ret=(cr_ref, sr_ref, DKA, DKA // 2),
                  r32=(c32_ref, s32_ref, DC, DC // ROPE_FRAC // 2),
                  r64=(c64_ref, s64_ref, DH, DH // ROPE_FRAC // 2))

    def group(name):
        start, width = _GRP[name]
        p = _dot(h, w_ref[:, start:start + width])
        kind = _ROPE_KIND.get(name)
        if kind is not None:
            c_ref, s_ref, headw, half = tables[kind]
            lane = lax.broadcasted_iota(I32, (1, width), 1)
            first = (lane & (headw - 1)) < half
            rx = jnp.where(first, pltpu.roll(p, width - half, axis=1), pltpu.roll(p, half, axis=1))
            p = p * c_ref[:, :width] + rx * s_ref[:, :width]
        if name in _SCALE:
            p = p * _SCALE[name]
        return p

    comb = None
    for name, layout, dtype in outs:
        ref = out_refs[name]
        if name in ('comb', 'auxT'):
            if comb is None:
                a = group('aux')
                z = a + bf_ref[...]
                lf = jnp.minimum(z, 0.0) - jnp.log1p(jnp.exp(-jnp.abs(z)))
                lane = lax.broadcasted_iota(I32, (1, LANES), 1)
                comb = jnp.where(lane < N_HEADS, lf, a)
                if with_cumsum:
                    carry_ref = refs[-1]

                    @pl.when(pl.program_id(0) % tiles_per_seq == 0)
                    def _():
                        carry_ref[...] = jnp.zeros_like(carry_ref)

                    r = lax.broadcasted_iota(I32, (tm, tm), 0)
                    c = lax.broadcasted_iota(I32, (tm, tm), 1)
                    tri = jnp.where(c <= r, 1.0, 0.0).astype(BF16)
                    l1, l2, l3 = _split3(lf)
                    cs = _dot(tri, l1) + _dot(tri, l2) + _dot(tri, l3) + carry_ref[...]
                    carry_ref[...] = cs[tm - 1:tm, :]
                    comb = jnp.where(lane < 2 * N_HEADS, comb,
                                     jnp.where(lane < 3 * N_HEADS, pltpu.roll(cs, 2 * N_HEADS, axis=1), 0.0))
            val = comb
        else:
            val = group(name)
        if layout == 'tok':
            ref[...] = val.astype(dtype)
        else:
            vt = val.T
            rows = _T_ROWS.get(name, vt.shape[0])
            ref[...] = vt[:rows, :].astype(dtype)


def _proj(x2d, gain, w_pack, bf_pad, tabs, *, mode, batch, seq, tm):
    n = x2d.shape[0]
    outs = _PROMPT_OUTS if mode == 'prompt' else _SAMPLE_OUTS
    tps = seq // tm if mode == 'prompt' else 1
    n_tiles = n // tm
    tab_rows = tabs[0].shape[0]
    tab_tiles = tab_rows // tm
    in_specs = [pl.BlockSpec((tm, D_MODEL), lambda i: (i, 0)),
                pl.BlockSpec((1, D_MODEL), lambda i: (0, 0)),
                pl.BlockSpec((D_MODEL, N_PACK), lambda i: (0, 0)),
                pl.BlockSpec((1, LANES), lambda i: (0, 0))]
    in_specs += [pl.BlockSpec((tm, 256), lambda i: (i % tab_tiles, 0)) for _ in range(6)]
    out_specs, out_shapes = [], []
    for name, layout, dtype in outs:
        width = LANES if name in ('comb', 'auxT', 'ki') else _GRP[name][1]
        if layout == 'tok':
            out_specs.append(pl.BlockSpec((tm, width), lambda i: (i, 0)))
            out_shapes.append(jax.ShapeDtypeStruct((n, width), dtype))
        else:
            rows = _T_ROWS.get(name, width)
            if mode == 'prompt':
                out_specs.append(pl.BlockSpec((None, rows, tm), lambda i: (i // tps, 0, i % tps)))
                out_shapes.append(jax.ShapeDtypeStruct((batch, rows, seq), dtype))
            else:
                out_specs.append(pl.BlockSpec((rows, tm), lambda i: (0, i)))
                out_shapes.append(jax.ShapeDtypeStruct((rows, n), dtype))
    res = pl.pallas_call(
        functools.partial(_proj_kernel, outs=outs, with_cumsum=(mode == 'prompt'), tiles_per_seq=tps),
        grid=(n_tiles,), in_specs=in_specs, out_specs=out_specs, out_shape=out_shapes,
        scratch_shapes=[pltpu.VMEM((1, LANES), F32)],
        compiler_params=_cparams(("arbitrary",)), name="proj_" + mode,
    )(x2d, gain, w_pack, bf_pad, *tabs)
    return dict(zip([o[0] for o in outs], res))


def _ret_finish(o, ga_h):
    return _rms(o) * (ga_h * jax.nn.sigmoid(ga_h))


def _ret_prompt_kernel(q_ref, kT_ref, v_ref, ga_ref, o_ref, st_ref, r_ref):
    c = q_ref.shape[0]
    j = pl.program_id(1)

    @pl.when(j == 0)
    def _():
        r_ref[...] = jnp.zeros_like(r_ref)

    q = q_ref[...]
    row = lax.broadcasted_iota(I32, (c, c), 0)
    col = lax.broadcasted_iota(I32, (c, c), 1)
    rel = (row - col).astype(F32)
    irow = lax.broadcasted_iota(I32, (c, 1), 0).astype(F32)
    icol = lax.broadcasted_iota(I32, (1, c), 1).astype(F32)
    for h in range(N_HEADS):
        lg = LOG_GAMMA[h]
        dmask = jnp.where(rel >= 0, jnp.exp(jnp.maximum(rel, 0.0) * lg), 0.0)
        xi = jnp.exp((irow + 1.0) * lg)
        zeta = jnp.exp((c - 1.0 - icol) * lg)
        qp, p = _masked_pair(q, h, DKA)
        s = _dot(qp, kT_ref[p * LANES:(p + 1) * LANES, :]) * dmask
        v_h = v_ref[:, h * DVA:(h + 1) * DVA]
        r_pair = r_ref[p * LANES:(p + 1) * LANES, :].astype(BF16)
        o = _dot(s.astype(BF16), v_h) + _dot((qp.astype(F32) * xi).astype(BF16), r_pair)
        kz = (kT_ref[h * DKA:(h + 1) * DKA, :].astype(F32) * zeta).astype(BF16)
        r_ref[h * DKA:(h + 1) * DKA, :] = math.exp(c * lg) * r_ref[h * DKA:(h + 1) * DKA, :] + _dot(kz, v_h)
        o_ref[:, h * DVA:(h + 1) * DVA] = _ret_finish(o, ga_ref[:, h * DVA:(h + 1) * DVA]).astype(o_ref.dtype)

    @pl.when(j == pl.num_programs(1) - 1)
    def _():
        st_ref[...] = r_ref[...]


def _ret_prompt(qa, kaT, va, ga, *, batch, seq):
    c = RET_CHUNK
    nc = seq // c
    return pl.pallas_call(
        _ret_prompt_kernel, grid=(batch, nc),
        in_specs=[pl.BlockSpec((c, 256), lambda b, j: (b * nc + j, 0)),
                  pl.BlockSpec((None, 256, c), lambda b, j: (b, 0, j)),
                  pl.BlockSpec((c, 512), lambda b, j: (b * nc + j, 0)),
                  pl.BlockSpec((c, 512), lambda b, j: (b * nc + j, 0))],
        out_specs=[pl.BlockSpec((c, 512), lambda b, j: (b * nc + j, 0)),
                   pl.BlockSpec((None, 256, DVA), lambda b, j: (b, 0, 0))],
        out_shape=[jax.ShapeDtypeStruct((batch * seq, 512), BF16),
                   jax.ShapeDtypeStruct((batch, 256, DVA), F32)],
        scratch_shapes=[pltpu.VMEM((256, DVA), F32)],
        compiler_params=_cparams(("parallel", "arbitrary")), name="ret_prompt",
    )(qa, kaT, va, ga)


def _ret_sample_kernel(q_ref, kT_ref, v_ref, ga_ref, st_ref, o_ref, stn_ref, *, batch, seq):
    n = batch * seq
    q = q_ref[...]
    row = lax.broadcasted_iota(I32, (n, n), 0)
    col = lax.broadcasted_iota(I32, (n, n), 1)
    same = (row // seq) == (col // seq)
    rel = ((row % seq) - (col % seq)).astype(F32)
    irow = (lax.broadcasted_iota(I32, (n, 1), 0) % seq).astype(F32)
    icol_i = lax.broadcasted_iota(I32, (1, n), 1)
    icol = (icol_i % seq).astype(F32)
    seq_lane = lax.broadcasted_iota(I32, (1, batch * LANES), 1) // LANES
    tok_row = lax.broadcasted_iota(I32, (n, 1), 0) // seq
    own_state = seq_lane == tok_row
    seq_sub = lax.broadcasted_iota(I32, (batch * DKA, 1), 0) // DKA
    own_tok = seq_sub == (icol_i // seq)
    for h in range(N_HEADS):
        lg = LOG_GAMMA[h]
        dmask = jnp.where(same & (rel >= 0), jnp.exp(jnp.maximum(rel, 0.0) * lg), 0.0)
        xi = jnp.exp((irow + 1.0) * lg)
        zeta = jnp.exp((seq - 1.0 - icol) * lg)
        qp, p = _masked_pair(q, h, DKA)
        kT_pair = kT_ref[p * LANES:(p + 1) * LANES, :].astype(BF16)
        s = _dot(qp.astype(BF16), kT_pair) * dmask
        v_h = v_ref[:, h * DVA:(h + 1) * DVA].astype(BF16)
        qx = (qp * xi).astype(BF16)
        q_big = jnp.where(own_state, jnp.concatenate([qx] * batch, axis=1), jnp.zeros((), BF16))
        r_pair = st_ref[:, p * LANES:(p + 1) * LANES, :].reshape(batch * LANES, DVA).astype(BF16)
        o = _dot(s.astype(BF16), v_h) + _dot(q_big, r_pair)
        o_ref[:, h * DVA:(h + 1) * DVA] = _ret_finish(o, ga_ref[:, h * DVA:(h + 1) * DVA]).astype(o_ref.dtype)
        kz = (kT_ref[h * DKA:(h + 1) * DKA, :] * zeta).astype(BF16)
        k_big = jnp.where(own_tok, jnp.concatenate([kz] * batch, axis=0), jnp.zeros((), BF16))
        upd = _dot(k_big, v_h).reshape(batch, DKA, DVA)
        stn_ref[:, h * DKA:(h + 1) * DKA, :] = math.exp(seq * lg) * st_ref[:, h * DKA:(h + 1) * DKA, :] + upd


def _ret_sample(qa, kaT, va, ga, state, *, batch, seq):
    n = batch * seq
    full = lambda shape: pl.BlockSpec(shape, lambda i: (0,) * len(shape))
    return pl.pallas_call(
        functools.partial(_ret_sample_kernel, batch=batch, seq=seq), grid=(1,),
        in_specs=[full((n, 256)), full((256, n)), full((n, 512)), full((n, 512)), full((batch, 256, DVA))],
        out_specs=[full((n, 512)), full((batch, 256, DVA))],
        out_shape=[jax.ShapeDtypeStruct((n, 512), BF16), jax.ShapeDtypeStruct((batch, 256, DVA), F32)],
        compiler_params=_cparams(("arbitrary",)), name="ret_sample",
    )(qa, kaT, va, ga, state)


def _lam_value(lam_ref, lam_init):
    l4 = lam_ref[...]
    s1 = jnp.sum(l4[0:1, :] * l4[1:2, :], axis=1, keepdims=True)
    s2 = jnp.sum(l4[2:3, :] * l4[3:4, :], axis=1, keepdims=True)
    return jnp.exp(s1) - jnp.exp(s2) + lam_init


def _online(s, mask, m, l):
    if mask is not None:
        s = jnp.where(mask, s, NEG)
    m2 = jnp.maximum(m, jnp.max(s, axis=1, keepdims=True))
    alpha = jnp.exp(m - m2)
    p = jnp.exp(s - m2)
    if mask is not None:
        p = jnp.where(mask, p, 0.0)
    return p, m2, alpha, l * alpha + jnp.sum(p, axis=1, keepdims=True)


def _upper_tri(n):
    r = lax.broadcasted_iota(I32, (n, n), 0)
    c = lax.broadcasted_iota(I32, (n, n), 1)
    return jnp.where(r <= c, 1.0, 0.0).astype(BF16)


def _attn_prompt_kernel(*refs, mode, tb, k_sel, lam_init):
    i = pl.program_id(1)
    if mode == 'fox':
        q_ref, kT_ref, vT_ref, comb_ref, auxT_ref, o_ref = refs
    elif mode == 'diff':
        q_ref, kT_ref, vT_ref, lam_ref, gd_ref, o_ref = refs
    else:
        q_ref, kT_ref, vT_ref, qi_ref, kiT_ref, comb_ref, o_ref, sk_ref = refs
    q = q_ref[...]
    row = lax.broadcasted_iota(I32, (tb, tb), 0)
    col = lax.broadcasted_iota(I32, (tb, tb), 1)
    causal = col <= row

    def blk(j):
        return pl.ds(pl.multiple_of(j * tb, tb), tb)

    if mode == 'dsa':
        qi = qi_ref[...]
        comb = comb_ref[...]
        qips = [_masked_pair(qi, h, DI)[0] for h in range(N_HEADS)]
        wcols = [comb[:, N_HEADS + h:N_HEADS + h + 1] * (DI ** -0.5 * N_HEADS ** -0.5) for h in range(N_HEADS)]

        def score_block(j, diag):
            kiT = kiT_ref[:, blk(j)].astype(BF16)
            kiT2 = jnp.concatenate([kiT, kiT], axis=0)
            sc = jnp.zeros((tb, tb), F32)
            for h in range(N_HEADS):
                sc = sc + jnp.maximum(_dot(qips[h], kiT2), 0.0) * wcols[h]
            if diag:
                sc = jnp.where(causal, sc, -jnp.inf)
            sk_ref[j] = _order_key(sc)

        def score_body(j, carry):
            score_block(j, False)
            return carry

        lax.fori_loop(0, i, score_body, 0)
        score_block(i, True)

        def count_ge(cand):
            def body(j, acc):
                ind = jnp.where(sk_ref[j] >= cand, 1.0, 0.0)
                for c in range(tb // LANES):
                    acc = acc + ind[:, c * LANES:(c + 1) * LANES]
                return acc
            acc = lax.fori_loop(0, i + 1, body, jnp.zeros((tb, LANES), F32))
            return jnp.sum(acc, axis=1, keepdims=True)

        thr = _kth_largest_key(count_ge, float(k_sel), (tb, 1))

        def gt_body(j, acc):
            ind = jnp.where(sk_ref[j] > thr, 1.0, 0.0)
            for c in range(tb // LANES):
                acc = acc + ind[:, c * LANES:(c + 1) * LANES]
            return acc
        n_gt = jnp.sum(lax.fori_loop(0, i + 1, gt_body, jnp.zeros((tb, LANES), F32)), axis=1, keepdims=True)
        need = float(k_sel) - n_gt
        tri = _upper_tri(tb)

        def sel_block(j, seen, diag):
            x = sk_ref[j]
            eq = x == thr
            rank = seen + _dot(jnp.where(eq, 1.0, 0.0).astype(BF16), tri)
            sel = (x > thr) | (eq & (rank <= need))
            if diag:
                sel = sel & causal
            sk_ref[j] = jnp.where(sel, 1, 0).astype(I32)
            return rank[:, tb - 1:tb]

        seen = lax.fori_loop(0, i, lambda j, s: sel_block(j, s, False), jnp.zeros((tb, 1), F32))
        sel_block(i, seen, True)

    n_maps = 2 * N_HEADS if mode == 'diff' else N_HEADS
    w = 256 // n_maps
    results = []
    for g in range(n_maps):
        hv = g // 2 if mode == 'diff' else g
        qp, p = _masked_pair(q, g, w)
        if mode == 'fox':
            cq = comb_ref[:, 2 * N_HEADS + g:2 * N_HEADS + g + 1]

        def step(j, carry, diag):
            m, l, acc = carry
            s = _dot(qp, kT_ref[p * LANES:(p + 1) * LANES, blk(j)].astype(BF16))
            if mode == 'fox':
                s = s + (cq - auxT_ref[2 * N_HEADS + g:2 * N_HEADS + g + 1, blk(j)])
            mask = causal if diag else None
            if mode == 'dsa':
                mask = sk_ref[j] != 0
            pr, m2, alpha, l2 = _online(s, mask, m, l)
            vT = vT_ref[hv * DH:(hv + 1) * DH, blk(j)].astype(BF16)
            return m2, l2, acc * alpha + _dot_nt(pr.astype(BF16), vT)

        init = (jnp.full((tb, 1), NEG, F32), jnp.zeros((tb, 1), F32), jnp.zeros((tb, DH), F32))
        carry = lax.fori_loop(0, i, lambda j, c: step(j, c, False), init)
        m, l, acc = step(i, carry, True)
        results.append(acc / l)

    if mode == 'diff':
        lam = _lam_value(lam_ref, lam_init)
        for h in range(N_HEADS):
            o = results[2 * h] - lam * results[2 * h + 1]
            o = _rms(o, gd_ref[:, h * DH:(h + 1) * DH]) * (1.0 - lam_init)
            o_ref[:, h * DH:(h + 1) * DH] = o.astype(o_ref.dtype)
    else:
        for h in range(N_HEADS):
            o_ref[:, h * DH:(h + 1) * DH] = results[h].astype(o_ref.dtype)


def _attn_prompt(mode, q, kT, vT, extra, *, batch, seq, tb, lam_init=0.0):
    nq = seq // tb
    tile = lambda width: pl.BlockSpec((tb, width), lambda b, i: (b * nq + i, 0))
    resident = lambda rows: pl.BlockSpec((None, rows, seq), lambda b, i: (b, 0, 0))
    in_specs = [tile(256), resident(256), resident(256)]
    scratch = []
    if mode == 'fox':
        in_specs += [tile(LANES), resident(16)]
    elif mode == 'diff':
        in_specs += [pl.BlockSpec((4, DC), lambda b, i: (0, 0)), pl.BlockSpec((1, 256), lambda b, i: (0, 0))]
    else:
        in_specs += [tile(256), resident(DI), tile(LANES)]
        scratch = [pltpu.VMEM((nq, tb, tb), I32)]
    return pl.pallas_call(
        functools.partial(_attn_prompt_kernel, mode=mode, tb=tb, k_sel=min(TOPK_MAX, seq // 4), lam_init=lam_init),
        grid=(batch, nq), in_specs=in_specs, out_specs=tile(256),
        out_shape=jax.ShapeDtypeStruct((batch * seq, 256), BF16), scratch_shapes=scratch,
        compiler_params=_cparams(("parallel", "arbitrary")), name=mode + "_prompt",
    )(q, kT, vT, *extra)


def _pad_rows(x, rows):
    return jnp.concatenate([x, jnp.zeros((rows - x.shape[0], x.shape[1]), x.dtype)], axis=0)


def _tile_rows(x, reps):
    return jnp.concatenate([x] * reps, axis=0)


def _prefix_rows(x):
    n = x.shape[0]
    rid = lax.broadcasted_iota(I32, (n, 1), 0)
    out = jnp.zeros_like(x)
    for u in range(n):
        out = out + jnp.where(rid >= u, x[u:u + 1, :], 0.0)
    return out


def _attn_sample_kernel(*refs, mode, group, n_pages, seq, lam_init):
    pt_ref = refs[0]
    refs = refs[1:]
    q_ref, kn_ref, vn_ref = refs[:3]
    k_refs = refs[3:3 + group]
    v_refs = refs[3 + group:3 + 2 * group]
    rest = refs[3 + 2 * group:]
    if mode == 'fox':
        lf_refs = rest[:group]
        combn_ref, o_ref, qx_ref, m_ref, l_ref, acc_ref, sfx_ref = rest[group:]
    elif mode == 'diff':
        lam_ref, gd_ref, o_ref, qx_ref, m_ref, l_ref, acc_ref = rest
    else:
        sel_ref, o_ref, qx_ref, m_ref, l_ref, acc_ref = rest
    del pt_ref
    n_maps = 2 * N_HEADS if mode == 'diff' else N_HEADS
    w = 256 // n_maps
    rows = n_maps * seq
    s_idx = pl.program_id(1)
    n_steps = n_pages // group

    @pl.when(s_idx == 0)
    def _():
        q = q_ref[...]
        for g in range(n_maps):
            qx_ref[g * seq:(g + 1) * seq, :] = jnp.where(_lane_mask(256, g * w, (g + 1) * w), q, 0.0)
        m_ref[...] = jnp.full_like(m_ref, NEG)
        l_ref[...] = jnp.zeros_like(l_ref)
        acc_ref[...] = jnp.zeros_like(acc_ref)
        if mode == 'fox':
            sfx_ref[...] = jnp.zeros_like(sfx_ref)

    def new_prefix():
        cn = _prefix_rows(combn_ref[...])
        return cn, jnp.concatenate([cn[:, h:h + 1] for h in range(N_HEADS)], axis=0)

    def update(s_parts, mask, v_parts, nt):
        s = s_parts[0] if len(s_parts) == 1 else jnp.concatenate(s_parts, axis=1)
        pr, m2, alpha, l2 = _online(s, mask, m_ref[...], l_ref[...])
        m_ref[...] = m2
        l_ref[...] = l2
        pv = None
        for g, v in enumerate(v_parts):
            pg = pr[:, g * LANES:(g + 1) * LANES].astype(BF16)
            t = _dot_nt(pg, v) if nt else _dot(pg, v)
            pv = t if pv is None else pv + t
        acc_ref[...] = acc_ref[...] * alpha + pv

    @pl.when(s_idx < n_steps)
    def _():
        qx = qx_ref[...].astype(BF16)
        s_parts, v_parts, masks = [], [], []
        if mode == 'fox':
            r = lax.broadcasted_iota(I32, (LANES, LANES), 0)
            c = lax.broadcasted_iota(I32, (LANES, LANES), 1)
            later = jnp.where(r > c, 1.0, 0.0).astype(BF16)
            _, cq = new_prefix()
            sfx = sfx_ref[...]
        for g in range(group):
            s = _dot(qx, k_refs[g][...].astype(BF16))
            if mode == 'fox':
                lf = lf_refs[g][...]
                lf_rows = jnp.concatenate([jnp.broadcast_to(lf[h:h + 1, :], (seq, LANES)) for h in range(N_HEADS)], axis=0)
                a1, a2, a3 = _split3(lf_rows)
                after = _dot(a1, later) + _dot(a2, later) + _dot(a3, later) + sfx
                s = s + (cq + after)
                sfx = sfx + jnp.sum(lf_rows, axis=1, keepdims=True)
            if mode == 'dsa':
                masks.append(_tile_rows(sel_ref[s_idx * group + g], n_maps))
            s_parts.append(s)
            v_parts.append(v_refs[g][...].astype(BF16))
        if mode == 'fox':
            sfx_ref[...] = sfx
        mask = (jnp.concatenate(masks, axis=1) != 0) if masks else None
        update(s_parts, mask, v_parts, True)

    @pl.when(s_idx == n_steps)
    def _():
        qx = qx_ref[...].astype(BF16)
        kpad = _pad_rows(kn_ref[...], LANES).astype(BF16)
        vpad = _pad_rows(vn_ref[...], LANES).astype(BF16)
        s = _dot_nt(qx, kpad)
        qi = lax.broadcasted_iota(I32, (rows, LANES), 0) % seq
        kj = lax.broadcasted_iota(I32, (rows, LANES), 1)
        mask = kj <= qi
        if mode == 'fox':
            cn, cq = new_prefix()
            cnT = _pad_rows(cn, LANES).T
            ck = jnp.concatenate([jnp.broadcast_to(cnT[h:h + 1, :], (seq, LANES)) for h in range(N_HEADS)], axis=0)
            s = s + (cq - ck)
        if mode == 'dsa':
            mask = mask & (_tile_rows(sel_ref[n_pages], n_maps) != 0)
        update([s], mask, [vpad], False)
        res = acc_ref[...] / l_ref[...]
        out = jnp.zeros((seq, 256), F32)
        if mode == 'diff':
            lam = _lam_value(lam_ref, lam_init)
            for h in range(N_HEADS):
                o = res[2 * h * seq:(2 * h + 1) * seq, :] - lam * res[(2 * h + 1) * seq:(2 * h + 2) * seq, :]
                hm = _lane_mask(256, h * DH, (h + 1) * DH)
                ms = jnp.sum(jnp.where(hm, o * o, 0.0), axis=1, keepdims=True) * (1.0 / DH)
                o = o * lax.rsqrt(ms + EPS) * gd_ref[...] * (1.0 - lam_init)
                out = out + jnp.where(hm, o, 0.0)
        else:
            for h in range(N_HEADS):
                out = out + jnp.where(_lane_mask(256, h * DH, (h + 1) * DH), res[h * seq:(h + 1) * seq, :], 0.0)
        o_ref[...] = out.astype(o_ref.dtype)


def _page_spec(rows, layer, g, group, n_pages, reverse):
    def imap(b, s, pt):
        page = jnp.minimum(s, n_pages // group - 1) * group + g
        if reverse:
            page = n_pages - 1 - page
        return (layer, pt[b, page], 0, 0)
    return pl.BlockSpec((None, None, rows, PAGE_SIZE), imap)


def _attn_sample(mode, page_table, q, kn, vn, cache_kT, cache_vT, extra, *, layer, batch, seq, group, lam_init=0.0):
    n_pages = page_table.shape[1]
    n_steps = n_pages // group
    n_maps = 2 * N_HEADS if mode == 'diff' else N_HEADS
    rows = n_maps * seq
    reverse = mode == 'fox'
    per_b = lambda width: pl.BlockSpec((None, seq, width), lambda b, s, pt: (b, 0, 0))
    in_specs = [per_b(256), per_b(256), per_b(256)]
    in_specs += [_page_spec(256, layer, g, group, n_pages, reverse) for g in range(group)]
    in_specs += [_page_spec(256, layer, g, group, n_pages, reverse) for g in range(group)]
    args = [q, kn, vn] + [cache_kT] * group + [cache_vT] * group
    scratch = [pltpu.VMEM((rows, 256), F32), pltpu.VMEM((rows, 1), F32), pltpu.VMEM((rows, 1), F32),
               pltpu.VMEM((rows, 256), F32)]
    if mode == 'fox':
        cache_lfT, comb_new = extra
        in_specs += [_page_spec(N_HEADS, layer, g, group, n_pages, True) for g in range(group)]
        in_specs += [per_b(LANES)]
        args += [cache_lfT] * group + [comb_new]
        scratch += [pltpu.VMEM((rows, 1), F32)]
    elif mode == 'diff':
        lam4, gd = extra
        in_specs += [pl.BlockSpec((4, DC), lambda b, s, pt: (0, 0)), pl.BlockSpec((1, 256), lambda b, s, pt: (0, 0))]
        args += [lam4, gd]
    else:
        (sel,) = extra
        in_specs += [pl.BlockSpec((None, n_pages + 1, seq, LANES), lambda b, s, pt: (b, 0, 0, 0))]
        args += [sel]
    return pl.pallas_call(
        functools.partial(_attn_sample_kernel, mode=mode, group=group, n_pages=n_pages, seq=seq, lam_init=lam_init),
        grid_spec=pltpu.PrefetchScalarGridSpec(
            num_scalar_prefetch=1, grid=(batch, n_steps + 1), in_specs=in_specs,
            out_specs=pl.BlockSpec((None, seq, 256), lambda b, s, pt: (b, 0, 0)), scratch_shapes=scratch),
        out_shape=jax.ShapeDtypeStruct((batch, seq, 256), F32),
        compiler_params=_cparams(("parallel", "arbitrary")), name=mode + "_sample",
    )(page_table, *args)


def _dsa_select_sample_kernel(*refs, group, n_pages, seq, k_sel):
    refs = refs[1:]
    qi_ref, kin_ref, combn_ref = refs[:3]
    ik_refs = refs[3:3 + group]
    sel_ref, sk_ref, qix_ref = refs[3 + group:]
    s_idx = pl.program_id(1)
    n_steps = n_pages // group

    @pl.when(s_idx == 0)
    def _():
        qi = qi_ref[...]
        for h in range(N_HEADS):
            qix_ref[h * seq:(h + 1) * seq, :] = _masked_pair(qi, h, DI)[0]

    def scores(logits):
        comb = combn_ref[...]
        sc = jnp.zeros((seq, LANES), F32)
        for h in range(N_HEADS):
            wcol = comb[:, N_HEADS + h:N_HEADS + h + 1] * (DI ** -0.5 * N_HEADS ** -0.5)
            sc = sc + jnp.maximum(logits[h * seq:(h + 1) * seq, :], 0.0) * wcol
        return sc

    @pl.when(s_idx < n_steps)
    def _():
        qix = qix_ref[...].astype(BF16)
        for g in range(group):
            ikT = ik_refs[g][...].astype(BF16)
            sk_ref[s_idx * group + g] = _order_key(scores(_dot(qix, jnp.concatenate([ikT, ikT], axis=0))))

    @pl.when(s_idx == n_steps)
    def _():
        qix = qix_ref[...].astype(BF16)
        kin = kin_ref[...]
        kin2 = _pad_rows(kin + pltpu.roll(kin, DI, axis=1), LANES).astype(BF16)
        sc = scores(_dot_nt(qix, kin2))
        qrow = lax.broadcasted_iota(I32, (seq, LANES), 0)
        kcol = lax.broadcasted_iota(I32, (seq, LANES), 1)
        sk_ref[n_pages] = _order_key(jnp.where(kcol <= qrow, sc, -jnp.inf))
        keys = sk_ref[...]

        def count_ge(cand):
            ind = jnp.where(keys >= cand[None], 1.0, 0.0)
            return jnp.sum(jnp.sum(ind, axis=0), axis=1, keepdims=True)

        thr = _kth_largest_key(count_ge, float(k_sel), (seq, 1))
        n_gt = jnp.sum(jnp.sum(jnp.where(keys > thr[None], 1.0, 0.0), axis=0), axis=1, keepdims=True)
        need = float(k_sel) - n_gt
        tri = _upper_tri(LANES)

        def body(j, seen):
            x = sk_ref[j]
            eq = x == thr
            eq16 = _pad_rows(jnp.where(eq, 1.0, 0.0), 16).astype(BF16)
            rank = seen + _dot(eq16, tri)[:seq, :]
            sel = (x > thr) | (eq & (rank <= need))
            sel_ref[j] = jnp.where(sel, 1, 0).astype(I32)
            return rank[:, LANES - 1:LANES]

        lax.fori_loop(0, n_pages + 1, body, jnp.zeros((seq, 1), F32))


def _dsa_select_sample(page_table, qi, ki_new, comb_new, cache_ikT, *, layer, batch, seq, group):
    n_pages = page_table.shape[1]
    n_steps = n_pages // group
    per_b = lambda width: pl.BlockSpec((None, seq, width), lambda b, s, pt: (b, 0, 0))
    in_specs = [per_b(256), per_b(LANES), per_b(LANES)]
    in_specs += [_page_spec(DI, layer, g, group, n_pages, False) for g in range(group)]
    k_sel = min(TOPK_MAX, (n_pages * PAGE_SIZE + seq) // 4)
    return pl.pallas_call(
        functools.partial(_dsa_select_sample_kernel, group=group, n_pages=n_pages, seq=seq, k_sel=k_sel),
        grid_spec=pltpu.PrefetchScalarGridSpec(
            num_scalar_prefetch=1, grid=(batch, n_steps + 1), in_specs=in_specs,
            out_specs=pl.BlockSpec((None, n_pages + 1, seq, LANES), lambda b, s, pt: (b, 0, 0, 0)),
            scratch_shapes=[pltpu.VMEM((n_pages + 1, seq, LANES), I32), pltpu.VMEM((N_HEADS * seq, LANES), F32)]),
        out_shape=jax.ShapeDtypeStruct((batch, n_pages + 1, seq, LANES), I32),
        compiler_params=_cparams(("parallel", "arbitrary")), name="dsa_select_sample",
    )(page_table, qi, ki_new, comb_new, *([cache_ikT] * group))


def _mix_kernel(x_ref, oa_ref, ob_ref, oc_ref, od_ref, gpre_ref, wg_ref, bg_ref, wpa_ref, wpb_ref, wpc_ref,
                wpd_ref, wo_ref, gpost_ref, y_ref):
    x = x_ref[...]
    h = _rms(x, gpre_ref[...]).astype(BF16)
    mixed = None
    for n, (o_ref, wp_ref) in enumerate(((oa_ref, wpa_ref), (ob_ref, wpb_ref), (oc_ref, wpc_ref), (od_ref, wpd_ref))):
        gate = jax.nn.sigmoid(_dot(h, wg_ref[:, n * D_MODEL:(n + 1) * D_MODEL]) + bg_ref[:, n * D_MODEL:(n + 1) * D_MODEL])
        t = gate * _dot(o_ref[...].astype(BF16), wp_ref[...])
        mixed = t if mixed is None else mixed + t
    y = _dot(mixed.astype(BF16), wo_ref[...])
    y_ref[...] = x + _rms(y, gpost_ref[...])


def _mix(x2d, oa, ob, oc, od, gpre, wg, bg, wpa, wpb, wpc, wpd, wo, gpost, *, tm):
    n = x2d.shape[0]
    tile = lambda width: pl.BlockSpec((tm, width), lambda i: (i, 0))
    const = lambda a: pl.BlockSpec(a.shape, lambda i: (0, 0))
    consts = (gpre, wg, bg, wpa, wpb, wpc, wpd, wo, gpost)
    return pl.pallas_call(
        _mix_kernel, grid=(n // tm,),
        in_specs=[tile(D_MODEL), tile(512), tile(256), tile(256), tile(256)] + [const(a) for a in consts],
        out_specs=tile(D_MODEL), out_shape=jax.ShapeDtypeStruct((n, D_MODEL), F32),
        compiler_params=_cparams(("parallel",)), name="mix",
    )(x2d, oa, ob, oc, od, *consts)


def _ffn_kernel(x_ref, gpre_ref, wgu_ref, wd_ref, gpost_ref, y_ref):
    x = x_ref[...]
    h = _rms(x, gpre_ref[...]).astype(BF16)
    g = _dot(h, wgu_ref[:, :D_FF])
    u = _dot(h, wgu_ref[:, D_FF:])
    a = (g * jax.nn.sigmoid(g) * u).astype(BF16)
    y_ref[...] = x + _rms(_dot(a, wd_ref[...]), gpost_ref[...])


def _ffn(x2d, gpre, wgu, wd, gpost, *, tm):
    n = x2d.shape[0]
    tile = pl.BlockSpec((tm, D_MODEL), lambda i: (i, 0))
    const = lambda a: pl.BlockSpec(a.shape, lambda i: (0, 0))
    consts = (gpre, wgu, wd, gpost)
    return pl.pallas_call(
        _ffn_kernel, grid=(n // tm,), in_specs=[tile] + [const(a) for a in consts],
        out_specs=tile, out_shape=jax.ShapeDtypeStruct((n, D_MODEL), F32),
        compiler_params=_cparams(("parallel",)), name="ffn",
    )(x2d, *consts)


def _rope_tables(pos):
    def one(headw, rot, theta):
        half = rot // 2
        inv = theta ** (-jnp.arange(half, dtype=F32) / half)
        ang = pos.astype(F32)[:, None] * inv[None, :]
        cos, sin = jnp.cos(ang), jnp.sin(ang)
        rest = headw - rot
        n = pos.shape[0]
        c = jnp.concatenate([cos, cos, jnp.ones((n, rest), F32)], axis=1)
        s = jnp.concatenate([-sin, sin, jnp.zeros((n, rest), F32)], axis=1)
        return jnp.tile(c, (1, 256 // headw)), jnp.tile(s, (1, 256 // headw))
    cr, sr = one(DKA, DKA, RET_THETA)
    c32, s32 = one(DC, DC // ROPE_FRAC, ROPE_THETA)
    c64, s64 = one(DH, DH // ROPE_FRAC, ROPE_THETA)
    return (cr, sr, c32, s32, c64, s64)


def _pack_w_in(w_in):
    depth = w_in.shape[0]
    seg = lambda name: w_in[:, :, _SRC[name][0]:_SRC[name][0] + _SRC[name][1]]
    zeros = lambda width: jnp.zeros((depth, D_MODEL, width), w_in.dtype)
    parts = [seg(n) for n in _PACK_ORDER]
    parts += [seg('ki'), zeros(LANES - DI), seg('fb'), seg('wi'), zeros(LANES - 2 * N_HEADS)]
    return jnp.concatenate(parts, axis=2).astype(BF16)


def _heads_T(a, batch, seq):
    return a.reshape(batch, N_HEADS, a.shape[1] // N_HEADS, seq).transpose(0, 3, 1, 2)


def kernel(x_prompt, x_sample, state_ret, cache_fox_k, cache_fox_v, cache_fox_lf, cache_diff_k, cache_diff_v, cache_dsa_k, cache_dsa_v, cache_dsa_ik, page_table, g_pre_mix, g_post_mix, g_pre_ffn, g_post_ffn, w_in, b_forget, lam_q1, lam_k1, lam_q2, lam_k2, g_diff, w_pa, w_pb, w_pc, w_pd, w_gate, b_gate, w_out, w_gu, w_down):
    bp, tp, _ = x_prompt.shape
    bs, ts, _ = x_sample.shape
    depth = w_in.shape[0]
    n_pool = cache_fox_k.shape[1]
    past = page_table.shape[1] * PAGE_SIZE
    tm_p = min(256, tp)
    tb = min(256, tp)
    group = min(8, page_table.shape[1])

    w_pack = _pack_w_in(w_in)
    bf_pad = jnp.pad(b_forget, ((0, 0), (0, LANES - N_HEADS)))[:, None, :]
    tabs_p = _rope_tables(jnp.arange(tp, dtype=I32))
    tabs_s = _rope_tables(jnp.tile(past + jnp.arange(ts, dtype=I32), bs))
    row = lambda a, l: a[l][None, :]
    bf = lambda a: a.astype(BF16)
    wg16, wpa16, wpb16, wpc16, wpd16, wo16, wgu16, wd16 = map(bf, (w_gate, w_pa, w_pb, w_pc, w_pd, w_out, w_gu, w_down))
    lam4 = jnp.stack([lam_q1, lam_k1, lam_q2, lam_k2], axis=1)
    pageT = lambda c: jnp.moveaxis(c.reshape(depth, n_pool, PAGE_SIZE, -1), 2, 3)
    fkT, fvT, ckT, cvT, dkT, dvT, ikT, lfT = map(pageT, (cache_fox_k, cache_fox_v, cache_diff_k, cache_diff_v,
                                                         cache_dsa_k, cache_dsa_v, cache_dsa_ik, cache_fox_lf))

    xp = x_prompt.reshape(bp * tp, D_MODEL)
    xs = x_sample.reshape(bs * ts, D_MODEL)
    new = {n: [] for n in ('ret_p', 'ret_s', 'fk_p', 'fv_p', 'flf_p', 'fk_s', 'fv_s', 'flf_s', 'ck_p', 'cv_p',
                           'ck_s', 'cv_s', 'dk_p', 'dv_p', 'dik_p', 'dk_s', 'dv_s', 'dik_s')}
    for l in range(depth):
        lam_init = 0.8 - 0.6 * math.exp(-0.3 * l)
        finish = lambda x, oa, ob, oc, od, tm: _ffn(
            _mix(x, oa, ob, oc, od, row(g_pre_mix, l), wg16[l], row(b_gate, l), wpa16[l], wpb16[l], wpc16[l],
                 wpd16[l], wo16[l], row(g_post_mix, l), tm=tm),
            row(g_pre_ffn, l), wgu16[l], wd16[l], row(g_post_ffn, l), tm=tm)
        pr = _proj(xp, row(g_pre_mix, l), w_pack[l], bf_pad[l], tabs_p, mode='prompt', batch=bp, seq=tp, tm=tm_p)
        oa, r_p = _ret_prompt(pr['qa'], pr['ka'], pr['va'], pr['ga'], batch=bp, seq=tp)
        ob = _attn_prompt('fox', pr['qb'], pr['kb'], pr['vb'], (pr['comb'], pr['auxT']), batch=bp, seq=tp, tb=tb)
        oc = _attn_prompt('diff', pr['qc'], pr['kc'], pr['vc'], (lam4[l], row(g_diff, l)), batch=bp, seq=tp, tb=tb,
                          lam_init=lam_init)
        od = _attn_prompt('dsa', pr['qd'], pr['kd'], pr['vd'], (pr['qi'], pr['ki'], pr['comb']), batch=bp, seq=tp, tb=tb)
        xp = finish(xp, oa, ob, oc, od, tm_p)
        new['ret_p'].append(r_p.reshape(bp, N_HEADS, DKA, DVA))
        for name, key in (('fk_p', 'kb'), ('fv_p', 'vb'), ('ck_p', 'kc'), ('cv_p', 'vc'), ('dk_p', 'kd'), ('dv_p', 'vd')):
            new[name].append(_heads_T(pr[key], bp, tp))
        new['flf_p'].append(jnp.swapaxes(pr['auxT'][:, :N_HEADS, :], 1, 2))
        new['dik_p'].append(jnp.swapaxes(pr['ki'], 1, 2))
        sm = _proj(xs, row(g_pre_mix, l), w_pack[l], bf_pad[l], tabs_s, mode='sample', batch=bs, seq=ts, tm=bs * ts)
        oa, r_s = _ret_sample(sm['qa'], sm['ka'], sm['va'], sm['ga'], state_ret[l].reshape(bs, 256, DVA), batch=bs, seq=ts)
        per_b = lambda a: a.reshape(bs, ts, a.shape[-1])
        common = dict(layer=l, batch=bs, seq=ts, group=group)
        ob = _attn_sample('fox', page_table, per_b(sm['qb']), per_b(sm['kb']), per_b(sm['vb']), fkT, fvT,
                          (lfT, per_b(sm['comb'])), **common)
        oc = _attn_sample('diff', page_table, per_b(sm['qc']), per_b(sm['kc']), per_b(sm['vc']), ckT, cvT,
                          (lam4[l], row(g_diff, l)), lam_init=lam_init, **common)
        sel = _dsa_select_sample(page_table, per_b(sm['qi']), per_b(sm['ki']), per_b(sm['comb']), ikT, **common)
        od = _attn_sample('dsa', page_table, per_b(sm['qd']), per_b(sm['kd']), per_b(sm['vd']), dkT, dvT, (sel,), **common)
        flat = lambda a: a.reshape(bs * ts, 256)
        xs = finish(xs, oa, flat(ob), flat(oc), flat(od), bs * ts)
        new['ret_s'].append(r_s.reshape(bs, N_HEADS, DKA, DVA))
        for name, key in (('fk_s', 'kb'), ('fv_s', 'vb'), ('ck_s', 'kc'), ('cv_s', 'vc'), ('dk_s', 'kd'), ('dv_s', 'vd')):
            new[name].append(sm[key].reshape(bs, ts, N_HEADS, DH))
        new['flf_s'].append(sm['comb'][:, :N_HEADS].reshape(bs, ts, N_HEADS))
        new['dik_s'].append(sm['ki'][:, :DI].reshape(bs, ts, DI))
    st = lambda n: jnp.stack(new[n], axis=0)
    return (xp.reshape(bp, tp, D_MODEL), xs.reshape(bs, ts, D_MODEL), st('ret_p'), st('ret_s'),
            st('fk_p'), st('fv_p'), st('flf_p'), st('fk_s'), st('fv_s'), st('flf_s'),
            st('ck_p'), st('cv_p'), st('ck_s'), st('cv_s'),
            st('dk_p'), st('dv_p'), st('dik_p'), st('dk_s'), st('dv_s'), st('dik_s'))
```

```python
import functools
import math

import jax
import jax.numpy as jnp
from jax import lax
from jax.experimental import pallas as pl
from jax.experimental.pallas import tpu as pltpu

F32 = jnp.float32
BF16 = jnp.bfloat16
I32 = jnp.int32

D_MODEL = 1024
HEAD_DIM = 64
N_HEADS = 4
DKA = HEAD_DIM
DVA = 2 * HEAD_DIM
RET_CHUNK = 128
RET_THETA = 10000.0
DH = HEAD_DIM
DC = HEAD_DIM // 2
DI = HEAD_DIM
TOPK_MAX = 256
ROPE_THETA = 500000.0
ROPE_FRAC = 4
D_FF = 2816
EPS = 1e-6
PAGE_SIZE = 128
LANES = 128
NEG = -1e30
INT_MIN = -2 ** 31
LOG2E = math.log2(math.e)
VMEM_LIMIT = 56 * 1024 * 1024

LOG_GAMMA = tuple(math.log1p(-(2.0 ** (-5.0 - h))) for h in range(N_HEADS))

_SRC = dict(qa=(0, 256), ka=(256, 256), va=(512, 512), ga=(1024, 512),
            qb=(1536, 256), kb=(1792, 256), vb=(2048, 256), fb=(2304, 4),
            qc=(2308, 256), kc=(2564, 256), vc=(2820, 256),
            qd=(3076, 256), kd=(3332, 256), vd=(3588, 256),
            qi=(3844, 256), ki=(4100, 64), wi=(4164, 4))
_PACK_ORDER = ('qa', 'ka', 'va', 'ga', 'qb', 'kb', 'vb', 'qc', 'kc', 'vc', 'qd', 'kd', 'vd', 'qi')
_GRP = {}
_off = 0
for _n in _PACK_ORDER:
    _GRP[_n] = (_off, _SRC[_n][1])
    _off += _SRC[_n][1]
_GRP['ki'] = (_off, LANES)
_off += LANES
_GRP['aux'] = (_off, LANES)
_off += LANES
N_PACK = _off


def _cparams(sem):
    return pltpu.CompilerParams(dimension_semantics=sem, vmem_limit_bytes=VMEM_LIMIT)


def _rms(x, g=None):
    y = x * lax.rsqrt(jnp.mean(x * x, axis=-1, keepdims=True) + EPS)
    return y if g is None else y * g


def _dot(a, b):
    return jnp.dot(a, b, preferred_element_type=F32)


def _dot_nt(a, b):
    return lax.dot_general(a, b, (((1,), (1,)), ((), ())), preferred_element_type=F32)


def _split3(x):
    x1 = x.astype(BF16)
    r1 = x - x1.astype(F32)
    x2 = r1.astype(BF16)
    x3 = (r1 - x2.astype(F32)).astype(BF16)
    return x1, x2, x3


def _lane_mask(width, lo, hi):
    lane = lax.broadcasted_iota(I32, (1, width), 1)
    return (lane >= lo) & (lane < hi)


def _masked_pair(q, g, w):
    p = (g * w) // LANES
    lo = g * w - p * LANES
    qp = q[:, p * LANES:(p + 1) * LANES]
    return jnp.where(_lane_mask(LANES, lo, lo + w), qp, jnp.zeros_like(qp)), p


def _order_key(score):
    bits = lax.bitcast_convert_type(score + 0.0, I32)
    return jnp.where(bits < 0, bits ^ jnp.int32(0x7FFFFFFF), bits)


def _kth_largest_key(count_ge, k, shape):
    nonneg = count_ge(jnp.zeros(shape, I32)) >= k
    lo0 = jnp.where(nonneg, jnp.int32(0), jnp.int32(INT_MIN))

    def body(b, lo):
        cand = lo + jnp.left_shift(jnp.int32(1), 30 - b)
        return jnp.where(count_ge(cand) >= k, cand, lo)

    return lax.fori_loop(0, 31, body, lo0)


_PROMPT_OUTS = (('qa', 'tok', BF16), ('ka', 'T', BF16), ('va', 'tok', BF16), ('ga', 'tok', F32),
                ('qb', 'tok', BF16), ('kb', 'T', F32), ('vb', 'T', F32),
                ('qc', 'tok', BF16), ('kc', 'T', F32), ('vc', 'T', F32),
                ('qd', 'tok', BF16), ('kd', 'T', F32), ('vd', 'T', F32),
                ('qi', 'tok', BF16), ('ki', 'T', F32), ('comb', 'tok', F32), ('auxT', 'T', F32))
_SAMPLE_OUTS = (('qa', 'tok', F32), ('ka', 'T', F32), ('va', 'tok', F32), ('ga', 'tok', F32),
                ('qb', 'tok', F32), ('kb', 'tok', F32), ('vb', 'tok', F32),
                ('qc', 'tok', F32), ('kc', 'tok', F32), ('vc', 'tok', F32),
                ('qd', 'tok', F32), ('kd', 'tok', F32), ('vd', 'tok', F32),
                ('qi', 'tok', F32), ('ki', 'tok', F32), ('comb', 'tok', F32))
_ROPE_KIND = dict(qa='ret', ka='ret', qc='r32', kc='r32', qd='r64', kd='r64', qi='r64', ki='r64')
_SCALE = dict(ka=DKA ** -0.5, qb=DH ** -0.5 * LOG2E, qc=DC ** -0.5 * LOG2E, qd=DH ** -0.5 * LOG2E)
_T_ROWS = dict(ki=DI, auxT=16)


def _proj_kernel(*refs, outs, with_cumsum, tiles_per_seq):
    (x_ref, g_ref, w_ref, bf_ref, cr_ref, sr_ref, c32_ref, s32_ref, c64_ref, s64_ref) = refs[:10]
    out_refs = dict(zip([o[0] for o in outs], refs[10:10 + len(outs)]))
    tm = x_ref.shape[0]
    h = _rms(x_ref[...], g_ref[...]).astype(BF16)
    tables = dict(ret=(cr_ref, sr_ref, DKA, DKA // 2),
                  r32=(c32_ref, s32_ref, DC, DC // ROPE_FRAC // 2),
                  r64=(c64_ref, s64_ref, DH, DH // ROPE_FRAC // 2))

    def group(name):
        start, width = _GRP[name]
        p = _dot(h, w_ref[:, start:start + width])
        kind = _ROPE_KIND.get(name)
        if kind is not None:
            c_ref, s_ref, headw, half = tables[kind]
            lane = lax.broadcasted_iota(I32, (1, width), 1)
            first = (lane & (headw - 1)) < half
            rx = jnp.where(first, pltpu.roll(p, width - half, axis=1), pltpu.roll(p, half, axis=1))
            p = p * c_ref[:, :width] + rx * s_ref[:, :width]
        if name in _SCALE:
            p = p * _SCALE[name]
        return p

    comb = None
    for name, layout, dtype in outs:
        ref = out_refs[name]
        if name in ('comb', 'auxT'):
            if comb is None:
                a = group('aux')
                z = a + bf_ref[...]
                lf = jnp.minimum(z, 0.0) - jnp.log1p(jnp.exp(-jnp.abs(z)))
                lane = lax.broadcasted_iota(I32, (1, LANES), 1)
                comb = jnp.where(lane < N_HEADS, lf, a)
                if with_cumsum:
                    carry_ref = refs[-1]

                    @pl.when(pl.program_id(0) % tiles_per_seq == 0)
                    def _():
                        carry_ref[...] = jnp.zeros_like(carry_ref)

                    r = lax.broadcasted_iota(I32, (tm, tm), 0)
                    c = lax.broadcasted_iota(I32, (tm, tm), 1)
                    tri = jnp.where(c <= r, 1.0, 0.0).astype(BF16)
                    l1, l2, l3 = _split3(lf)
                    cs = _dot(tri, l1) + _dot(tri, l2) + _dot(tri, l3) + carry_ref[...]
                    carry_ref[...] = cs[tm - 1:tm, :]
                    comb = jnp.where(lane < 2 * N_HEADS, comb,
                                     jnp.where(lane < 3 * N_HEADS, pltpu.roll(cs, 2 * N_HEADS, axis=1), 0.0))
            val = comb
        else:
            val = group(name)
        if layout == 'tok':
            ref[...] = val.astype(dtype)
        else:
            vt = val.T
            rows = _T_ROWS.get(name, vt.shape[0])
            ref[...] = vt[:rows, :].astype(dtype)


def _proj(x2d, gain, w_pack, bf_pad, tabs, *, mode, batch, seq, tm):
    n = x2d.shape[0]
    outs = _PROMPT_OUTS if mode == 'prompt' else _SAMPLE_OUTS
    tps = seq // tm if mode == 'prompt' else 1
    n_tiles = n // tm
    tab_rows = tabs[0].shape[0]
    tab_tiles = tab_rows // tm
    in_specs = [pl.BlockSpec((tm, D_MODEL), lambda i: (i, 0)),
                pl.BlockSpec((1, D_MODEL), lambda i: (0, 0)),
                pl.BlockSpec((D_MODEL, N_PACK), lambda i: (0, 0)),
                pl.BlockSpec((1, LANES), lambda i: (0, 0))]
    in_specs += [pl.BlockSpec((tm, 256), lambda i: (i % tab_tiles, 0)) for _ in range(6)]
    out_specs, out_shapes = [], []
    for name, layout, dtype in outs:
        width = LANES if name in ('comb', 'auxT', 'ki') else _GRP[name][1]
        if layout == 'tok':
            out_specs.append(pl.BlockSpec((tm, width), lambda i: (i, 0)))
            out_shapes.append(jax.ShapeDtypeStruct((n, width), dtype))
        else:
            rows = _T_ROWS.get(name, width)
            if mode == 'prompt':
                out_specs.append(pl.BlockSpec((None, rows, tm), lambda i: (i // tps, 0, i % tps)))
                out_shapes.append(jax.ShapeDtypeStruct((batch, rows, seq), dtype))
            else:
                out_specs.append(pl.BlockSpec((rows, tm), lambda i: (0, i)))
                out_shapes.append(jax.ShapeDtypeStruct((rows, n), dtype))
    res = pl.pallas_call(
        functools.partial(_proj_kernel, outs=outs, with_cumsum=(mode == 'prompt'), tiles_per_seq=tps),
        grid=(n_tiles,), in_specs=in_specs, out_specs=out_specs, out_shape=out_shapes,
        scratch_shapes=[pltpu.VMEM((1, LANES), F32)],
        compiler_params=_cparams(("arbitrary",)), name="proj_" + mode,
    )(x2d, gain, w_pack, bf_pad, *tabs)
    return dict(zip([o[0] for o in outs], res))


def _ret_finish(o, ga_h):
    return _rms(o) * (ga_h * jax.nn.sigmoid(ga_h))


def _ret_prompt_kernel(q_ref, kT_ref, v_ref, ga_ref, o_ref, st_ref, r_ref):
    c = q_ref.shape[0]
    j = pl.program_id(1)

    @pl.when(j == 0)
    def _():
        r_ref[...] = jnp.zeros_like(r_ref)

    q = q_ref[...]
    row = lax.broadcasted_iota(I32, (c, c), 0)
    col = lax.broadcasted_iota(I32, (c, c), 1)
    rel = (row - col).astype(F32)
    irow = lax.broadcasted_iota(I32, (c, 1), 0).astype(F32)
    icol = lax.broadcasted_iota(I32, (1, c), 1).astype(F32)
    for h in range(N_HEADS):
        lg = LOG_GAMMA[h]
        dmask = jnp.where(rel >= 0, jnp.exp(jnp.maximum(rel, 0.0) * lg), 0.0)
        xi = jnp.exp((irow + 1.0) * lg)
        zeta = jnp.exp((c - 1.0 - icol) * lg)
        qp, p = _masked_pair(q, h, DKA)
        s = _dot(qp, kT_ref[p * LANES:(p + 1) * LANES, :]) * dmask
        v_h = v_ref[:, h * DVA:(h + 1) * DVA]
        r_pair = r_ref[p * LANES:(p + 1) * LANES, :].astype(BF16)
        o = _dot(s.astype(BF16), v_h) + _dot((qp.astype(F32) * xi).astype(BF16), r_pair)
        kz = (kT_ref[h * DKA:(h + 1) * DKA, :].astype(F32) * zeta).astype(BF16)
        r_ref[h * DKA:(h + 1) * DKA, :] = math.exp(c * lg) * r_ref[h * DKA:(h + 1) * DKA, :] + _dot(kz, v_h)
        o_ref[:, h * DVA:(h + 1) * DVA] = _ret_finish(o, ga_ref[:, h * DVA:(h + 1) * DVA]).astype(o_ref.dtype)

    @pl.when(j == pl.num_programs(1) - 1)
    def _():
        st_ref[...] = r_ref[...]


def _ret_prompt(qa, kaT, va, ga, *, batch, seq):
    c = RET_CHUNK
    nc = seq // c
    return pl.pallas_call(
        _ret_prompt_kernel, grid=(batch, nc),
        in_specs=[pl.BlockSpec((c, 256), lambda b, j: (b * nc + j, 0)),
                  pl.BlockSpec((None, 256, c), lambda b, j: (b, 0, j)),
                  pl.BlockSpec((c, 512), lambda b, j: (b * nc + j, 0)),
                  pl.BlockSpec((c, 512), lambda b, j: (b * nc + j, 0))],
        out_specs=[pl.BlockSpec((c, 512), lambda b, j: (b * nc + j, 0)),
                   pl.BlockSpec((None, 256, DVA), lambda b, j: (b, 0, 0))],
        out_shape=[jax.ShapeDtypeStruct((batch * seq, 512), BF16),
                   jax.ShapeDtypeStruct((batch, 256, DVA), F32)],
        scratch_shapes=[pltpu.VMEM((256, DVA), F32)],
        compiler_params=_cparams(("parallel", "arbitrary")), name="ret_prompt",
    )(qa, kaT, va, ga)


def _ret_sample_kernel(q_ref, kT_ref, v_ref, ga_ref, st_ref, o_ref, stn_ref, *, batch, seq):
    n = batch * seq
    q = q_ref[...]
    row = lax.broadcasted_iota(I32, (n, n), 0)
    col = lax.broadcasted_iota(I32, (n, n), 1)
    same = (row // seq) == (col // seq)
    rel = ((row % seq) - (col % seq)).astype(F32)
    irow = (lax.broadcasted_iota(I32, (n, 1), 0) % seq).astype(F32)
    icol_i = lax.broadcasted_iota(I32, (1, n), 1)
    icol = (icol_i % seq).astype(F32)
    seq_lane = lax.broadcasted_iota(I32, (1, batch * LANES), 1) // LANES
    tok_row = lax.broadcasted_iota(I32, (n, 1), 0) // seq
    own_state = seq_lane == tok_row
    seq_sub = lax.broadcasted_iota(I32, (batch * DKA, 1), 0) // DKA
    own_tok = seq_sub == (icol_i // seq)
    for h in range(N_HEADS):
        lg = LOG_GAMMA[h]
        dmask = jnp.where(same & (rel >= 0), jnp.exp(jnp.maximum(rel, 0.0) * lg), 0.0)
        xi = jnp.exp((irow + 1.0) * lg)
        zeta = jnp.exp((seq - 1.0 - icol) * lg)
        qp, p = _masked_pair(q, h, DKA)
        kT_pair = kT_ref[p * LANES:(p + 1) * LANES, :].astype(BF16)
        s = _dot(qp.astype(BF16), kT_pair) * dmask
        v_h = v_ref[:, h * DVA:(h + 1) * DVA].astype(BF16)
        qx = (qp * xi).astype(BF16)
        q_big = jnp.where(own_state, jnp.concatenate([qx] * batch, axis=1), jnp.zeros((), BF16))
        r_pair = st_ref[:, p * LANES:(p + 1) * LANES, :].reshape(batch * LANES, DVA).astype(BF16)
        o = _dot(s.astype(BF16), v_h) + _dot(q_big, r_pair)
        o_ref[:, h * DVA:(h + 1) * DVA] = _ret_finish(o, ga_ref[:, h * DVA:(h + 1) * DVA]).astype(o_ref.dtype)
        kz = (kT_ref[h * DKA:(h + 1) * DKA, :] * zeta).astype(BF16)
        k_big = jnp.where(own_tok, jnp.concatenate([kz] * batch, axis=0), jnp.zeros((), BF16))
        upd = _dot(k_big, v_h).reshape(batch, DKA, DVA)
        stn_ref[:, h * DKA:(h + 1) * DKA, :] = math.exp(seq * lg) * st_ref[:, h * DKA:(h + 1) * DKA, :] + upd


def _ret_sample(qa, kaT, va, ga, state, *, batch, seq):
    n = batch * seq
    full = lambda shape: pl.BlockSpec(shape, lambda i: (0,) * len(shape))
    return pl.pallas_call(
        functools.partial(_ret_sample_kernel, batch=batch, seq=seq), grid=(1,),
        in_specs=[full((n, 256)), full((256, n)), full((n, 512)), full((n, 512)), full((batch, 256, DVA))],
        out_specs=[full((n, 512)), full((batch, 256, DVA))],
        out_shape=[jax.ShapeDtypeStruct((n, 512), BF16), jax.ShapeDtypeStruct((batch, 256, DVA), F32)],
        compiler_params=_cparams(("arbitrary",)), name="ret_sample",
    )(qa, kaT, va, ga, state)


def _lam_value(lam_ref, lam_init):
    l4 = lam_ref[...]
    s1 = jnp.sum(l4[0:1, :] * l4[1:2, :], axis=1, keepdims=True)
    s2 = jnp.sum(l4[2:3, :] * l4[3:4, :], axis=1, keepdims=True)
    return jnp.exp(s1) - jnp.exp(s2) + lam_init


def _online(s, mask, m, l):
    if mask is not None:
        s = jnp.where(mask, s, NEG)
    m2 = jnp.maximum(m, jnp.max(s, axis=1, keepdims=True))
    alpha = jnp.exp2(m - m2)
    p = jnp.exp2(s - m2)
    if mask is not None:
        p = jnp.where(mask, p, 0.0)
    return p, m2, alpha, l * alpha + jnp.sum(p, axis=1, keepdims=True)


def _upper_tri(n):
    r = lax.broadcasted_iota(I32, (n, n), 0)
    c = lax.broadcasted_iota(I32, (n, n), 1)
    return jnp.where(r <= c, 1.0, 0.0).astype(BF16)


def _attn_prompt_kernel(*refs, mode, tb, k_sel, lam_init):
    i = pl.program_id(1)
    m_ref, acc_ref = refs[-2:]
    refs = refs[:-2]
    if mode == 'fox':
        q_ref, kT_ref, vT_ref, comb_ref, auxT_ref, o_ref, cq_ref = refs
    elif mode == 'diff':
        q_ref, kT_ref, vT_ref, lam_ref, gd_ref, o_ref = refs
    else:
        q_ref, kT_ref, vT_ref, qi_ref, kiT_ref, comb_ref, o_ref, sk_ref = refs
    q = q_ref[...]
    row = lax.broadcasted_iota(I32, (tb, tb), 0)
    col = lax.broadcasted_iota(I32, (tb, tb), 1)
    causal = col <= row

    def blk(j):
        return pl.ds(pl.multiple_of(j * tb, tb), tb)

    if mode == 'dsa':
        qi = qi_ref[...]
        comb = comb_ref[...]
        qips = [_masked_pair(qi, h, DI)[0] for h in range(N_HEADS)]
        wcols = [comb[:, N_HEADS + h:N_HEADS + h + 1] * (DI ** -0.5 * N_HEADS ** -0.5) for h in range(N_HEADS)]

        def score_block(j, diag):
            kiT = kiT_ref[:, blk(j)].astype(BF16)
            kiT2 = jnp.concatenate([kiT, kiT], axis=0)
            sc = jnp.zeros((tb, tb), F32)
            for h in range(N_HEADS):
                sc = sc + jnp.maximum(_dot(qips[h], kiT2), 0.0) * wcols[h]
            if diag:
                sc = jnp.where(causal, sc, -jnp.inf)
            sk_ref[j] = _order_key(sc)

        def score_body(j, carry):
            score_block(j, False)
            return carry

        lax.fori_loop(0, i, score_body, 0)
        score_block(i, True)

        def count_ge(cand):
            def body(j, acc):
                ind = jnp.where(sk_ref[j] >= cand, 1.0, 0.0)
                for c in range(tb // LANES):
                    acc = acc + ind[:, c * LANES:(c + 1) * LANES]
                return acc
            acc = lax.fori_loop(0, i + 1, body, jnp.zeros((tb, LANES), F32))
            return jnp.sum(acc, axis=1, keepdims=True)

        thr = _kth_largest_key(count_ge, float(k_sel), (tb, 1))

        def gt_body(j, acc):
            ind = jnp.where(sk_ref[j] > thr, 1.0, 0.0)
            for c in range(tb // LANES):
                acc = acc + ind[:, c * LANES:(c + 1) * LANES]
            return acc
        n_gt = jnp.sum(lax.fori_loop(0, i + 1, gt_body, jnp.zeros((tb, LANES), F32)), axis=1, keepdims=True)
        need = float(k_sel) - n_gt
        tri = _upper_tri(tb)

        def sel_block(j, seen, diag):
            x = sk_ref[j]
            eq = x == thr
            rank = seen + _dot(jnp.where(eq, 1.0, 0.0).astype(BF16), tri)
            sel = (x > thr) | (eq & (rank <= need))
            if diag:
                sel = sel & causal
            sk_ref[j] = lax.bitcast_convert_type(jnp.where(sel, 0.0, NEG), I32)
            return rank[:, tb - 1:tb]

        seen = lax.fori_loop(0, i, lambda j, s: sel_block(j, s, False), jnp.zeros((tb, 1), F32))
        sel_block(i, seen, True)

    n_maps = 2 * N_HEADS if mode == 'diff' else N_HEADS
    w = 256 // n_maps
    qps = [_masked_pair(q, g, w) for g in range(n_maps)]
    if mode == 'fox':
        for g in range(n_maps):
            cq_ref[g] = jnp.broadcast_to(comb_ref[:, 2 * N_HEADS + g:2 * N_HEADS + g + 1] * LOG2E, (tb, LANES))
    m_ref[...] = jnp.full_like(m_ref, NEG)
    acc_ref[...] = jnp.zeros_like(acc_ref)
    ones_row = jnp.where(lax.broadcasted_iota(I32, (LANES - DH, tb), 0) == 0, 1.0, 0.0).astype(BF16)
    n_sub = tb // LANES

    def step(j, diag):
        kTs = [kT_ref[p * LANES:(p + 1) * LANES, blk(j)].astype(BF16) for p in range(2)]
        vTs = [jnp.concatenate([vT_ref[h * DH:(h + 1) * DH, blk(j)].astype(BF16), ones_row], axis=0)
               for h in range(N_HEADS)]
        if mode == 'dsa':
            bias = lax.bitcast_convert_type(sk_ref[j], F32)
        for g in range(n_maps):
            qp, p = qps[g]
            s = _dot(qp, kTs[p])
            if mode == 'fox':
                s = s - auxT_ref[2 * N_HEADS + g:2 * N_HEADS + g + 1, blk(j)] * LOG2E
            if mode == 'dsa':
                s = s + bias
            elif diag:
                s = jnp.where(causal, s, NEG)
            parts = [s[:, c * LANES:(c + 1) * LANES] for c in range(n_sub)]
            top = parts[0]
            for c in range(1, n_sub):
                top = jnp.maximum(top, parts[c])
            top = jnp.broadcast_to(jnp.max(top, axis=1, keepdims=True), (tb, LANES))
            m_old = m_ref[g]
            if mode == 'fox':
                m_new = jnp.maximum(m_old, top + cq_ref[g])
                shift = m_new - cq_ref[g]
            else:
                m_new = jnp.maximum(m_old, top)
                shift = m_new
            m_ref[g] = m_new
            pr = jnp.concatenate([jnp.exp2(part - shift).astype(BF16) for part in parts], axis=1)
            hv = g // 2 if mode == 'diff' else g
            acc_ref[g] = acc_ref[g] * jnp.exp2(m_old - m_new) + _dot_nt(pr, vTs[hv])

    def full_step(j, carry):
        step(j, False)
        return carry

    lax.fori_loop(0, i, full_step, 0)
    step(i, True)
    results = []
    for g in range(n_maps):
        acc = acc_ref[g]
        results.append(acc[:, :DH] / acc[:, DH:DH + 1])

    if mode == 'diff':
        lam = _lam_value(lam_ref, lam_init)
        for h in range(N_HEADS):
            o = results[2 * h] - lam * results[2 * h + 1]
            o = _rms(o, gd_ref[:, h * DH:(h + 1) * DH]) * (1.0 - lam_init)
            o_ref[:, h * DH:(h + 1) * DH] = o.astype(o_ref.dtype)
    else:
        for h in range(N_HEADS):
            o_ref[:, h * DH:(h + 1) * DH] = results[h].astype(o_ref.dtype)


def _attn_prompt(mode, q, kT, vT, extra, *, batch, seq, tb, lam_init=0.0):
    nq = seq // tb
    tile = lambda width: pl.BlockSpec((tb, width), lambda b, i: (b * nq + i, 0))
    resident = lambda rows: pl.BlockSpec((None, rows, seq), lambda b, i: (b, 0, 0))
    in_specs = [tile(256), resident(256), resident(256)]
    scratch = []
    if mode == 'fox':
        in_specs += [tile(LANES), resident(16)]
    elif mode == 'diff':
        in_specs += [pl.BlockSpec((4, DC), lambda b, i: (0, 0)), pl.BlockSpec((1, 256), lambda b, i: (0, 0))]
    else:
        in_specs += [tile(256), resident(DI), tile(LANES)]
        scratch = [pltpu.VMEM((nq, tb, tb), I32)]
    n_maps = 2 * N_HEADS if mode == 'diff' else N_HEADS
    if mode == 'fox':
        scratch = [pltpu.VMEM((n_maps, tb, LANES), F32)]
    scratch += [pltpu.VMEM((n_maps, tb, LANES), F32), pltpu.VMEM((n_maps, tb, LANES), F32)]
    return pl.pallas_call(
        functools.partial(_attn_prompt_kernel, mode=mode, tb=tb, k_sel=min(TOPK_MAX, seq // 4), lam_init=lam_init),
        grid=(batch, nq), in_specs=in_specs, out_specs=tile(256),
        out_shape=jax.ShapeDtypeStruct((batch * seq, 256), BF16), scratch_shapes=scratch,
        compiler_params=_cparams(("parallel", "arbitrary")), name=mode + "_prompt",
    )(q, kT, vT, *extra)


def _pad_rows(x, rows):
    return jnp.concatenate([x, jnp.zeros((rows - x.shape[0], x.shape[1]), x.dtype)], axis=0)


def _tile_rows(x, reps):
    return jnp.concatenate([x] * reps, axis=0)


def _prefix_rows(x):
    n = x.shape[0]
    rid = lax.broadcasted_iota(I32, (n, 1), 0)
    out = jnp.zeros_like(x)
    for u in range(n):
        out = out + jnp.where(rid >= u, x[u:u + 1, :], 0.0)
    return out


def _attn_sample_kernel(*refs, mode, group, n_pages, seq, lam_init):
    pt_ref = refs[0]
    refs = refs[1:]
    q_ref, kn_ref, vn_ref = refs[:3]
    k_refs = refs[3:3 + group]
    v_refs = refs[3 + group:3 + 2 * group]
    rest = refs[3 + 2 * group:]
    if mode == 'fox':
        lf_refs = rest[:group]
        combn_ref, o_ref, qx_ref, m_ref, l_ref, acc_ref, sfx_ref = rest[group:]
    elif mode == 'diff':
        lam_ref, gd_ref, o_ref, qx_ref, m_ref, l_ref, acc_ref = rest
    else:
        sel_ref, o_ref, qx_ref, m_ref, l_ref, acc_ref = rest
    del pt_ref
    n_maps = 2 * N_HEADS if mode == 'diff' else N_HEADS
    w = 256 // n_maps
    rows = n_maps * seq
    s_idx = pl.program_id(1)
    n_steps = n_pages // group

    @pl.when(s_idx == 0)
    def _():
        q = q_ref[...]
        for g in range(n_maps):
            qx_ref[g * seq:(g + 1) * seq, :] = jnp.where(_lane_mask(256, g * w, (g + 1) * w), q, 0.0)
        m_ref[...] = jnp.full_like(m_ref, NEG)
        l_ref[...] = jnp.zeros_like(l_ref)
        acc_ref[...] = jnp.zeros_like(acc_ref)
        if mode == 'fox':
            sfx_ref[...] = jnp.zeros_like(sfx_ref)

    def new_prefix():
        cn = _prefix_rows(combn_ref[...])
        return cn, jnp.concatenate([cn[:, h:h + 1] for h in range(N_HEADS)], axis=0)

    def update(s_parts, mask, v_parts, nt):
        s = s_parts[0] if len(s_parts) == 1 else jnp.concatenate(s_parts, axis=1)
        pr, m2, alpha, l2 = _online(s, mask, m_ref[...], l_ref[...])
        m_ref[...] = m2
        l_ref[...] = l2
        pv = None
        for g, v in enumerate(v_parts):
            pg = pr[:, g * LANES:(g + 1) * LANES].astype(BF16)
            t = _dot_nt(pg, v) if nt else _dot(pg, v)
            pv = t if pv is None else pv + t
        acc_ref[...] = acc_ref[...] * alpha + pv

    @pl.when(s_idx < n_steps)
    def _():
        qx = qx_ref[...].astype(BF16)
        s_parts, v_parts = [], []
        if mode == 'fox':
            r = lax.broadcasted_iota(I32, (LANES, LANES), 0)
            c = lax.broadcasted_iota(I32, (LANES, LANES), 1)
            later = jnp.where(r > c, 1.0, 0.0).astype(BF16)
            _, cq = new_prefix()
            lf_rows = jnp.concatenate(
                [jnp.broadcast_to(lf_refs[g][h:h + 1, :], (seq, LANES)) for g in range(group) for h in range(N_HEADS)],
                axis=0)
            a1, a2, a3 = _split3(lf_rows)
            within = _dot(a1, later) + _dot(a2, later) + _dot(a3, later)
            totals = jnp.sum(lf_rows, axis=1, keepdims=True)
            sfx = sfx_ref[...]
        for g in range(group):
            s = _dot(qx, k_refs[g][...].astype(BF16))
            if mode == 'fox':
                s = s + (cq + (within[g * rows:(g + 1) * rows, :] + sfx)) * LOG2E
                sfx = sfx + totals[g * rows:(g + 1) * rows, :]
            if mode == 'dsa':
                s = s + _tile_rows(sel_ref[s_idx * group + g], n_maps)
            s_parts.append(s)
            v_parts.append(v_refs[g][...].astype(BF16))
        if mode == 'fox':
            sfx_ref[...] = sfx
        update(s_parts, None, v_parts, True)

    @pl.when(s_idx == n_steps)
    def _():
        qx = qx_ref[...].astype(BF16)
        kpad = _pad_rows(kn_ref[...], LANES).astype(BF16)
        vpad = _pad_rows(vn_ref[...], LANES).astype(BF16)
        s = _dot_nt(qx, kpad)
        qi = lax.broadcasted_iota(I32, (rows, LANES), 0) % seq
        kj = lax.broadcasted_iota(I32, (rows, LANES), 1)
        mask = kj <= qi
        if mode == 'fox':
            cn, cq = new_prefix()
            cnT = _pad_rows(cn, LANES).T
            ck = jnp.concatenate([jnp.broadcast_to(cnT[h:h + 1, :], (seq, LANES)) for h in range(N_HEADS)], axis=0)
            s = s + (cq - ck) * LOG2E
        if mode == 'dsa':
            s = s + _tile_rows(sel_ref[n_pages], n_maps)
        update([s], mask, [vpad], False)
        res = acc_ref[...] / l_ref[...]
        out = jnp.zeros((seq, 256), F32)
        if mode == 'diff':
            lam = _lam_value(lam_ref, lam_init)
            for h in range(N_HEADS):
                o = res[2 * h * seq:(2 * h + 1) * seq, :] - lam * res[(2 * h + 1) * seq:(2 * h + 2) * seq, :]
                hm = _lane_mask(256, h * DH, (h + 1) * DH)
                ms = jnp.sum(jnp.where(hm, o * o, 0.0), axis=1, keepdims=True) * (1.0 / DH)
                o = o * lax.rsqrt(ms + EPS) * gd_ref[...] * (1.0 - lam_init)
                out = out + jnp.where(hm, o, 0.0)
        else:
            for h in range(N_HEADS):
                out = out + jnp.where(_lane_mask(256, h * DH, (h + 1) * DH), res[h * seq:(h + 1) * seq, :], 0.0)
        o_ref[...] = out.astype(o_ref.dtype)


def _page_spec(rows, layer, g, group, n_pages, reverse):
    def imap(b, s, pt):
        page = jnp.minimum(s, n_pages // group - 1) * group + g
        if reverse:
            page = n_pages - 1 - page
        return (layer, pt[b, page], 0, 0)
    return pl.BlockSpec((None, None, rows, PAGE_SIZE), imap)


def _attn_sample(mode, page_table, q, kn, vn, cache_kT, cache_vT, extra, *, layer, batch, seq, group, lam_init=0.0):
    n_pages = page_table.shape[1]
    n_steps = n_pages // group
    n_maps = 2 * N_HEADS if mode == 'diff' else N_HEADS
    rows = n_maps * seq
    reverse = mode == 'fox'
    per_b = lambda width: pl.BlockSpec((None, seq, width), lambda b, s, pt: (b, 0, 0))
    in_specs = [per_b(256), per_b(256), per_b(256)]
    in_specs += [_page_spec(256, layer, g, group, n_pages, reverse) for g in range(group)]
    in_specs += [_page_spec(256, layer, g, group, n_pages, reverse) for g in range(group)]
    args = [q, kn, vn] + [cache_kT] * group + [cache_vT] * group
    scratch = [pltpu.VMEM((rows, 256), F32), pltpu.VMEM((rows, 1), F32), pltpu.VMEM((rows, 1), F32),
               pltpu.VMEM((rows, 256), F32)]
    if mode == 'fox':
        cache_lfT, comb_new = extra
        in_specs += [_page_spec(N_HEADS, layer, g, group, n_pages, True) for g in range(group)]
        in_specs += [per_b(LANES)]
        args += [cache_lfT] * group + [comb_new]
        scratch += [pltpu.VMEM((rows, 1), F32)]
    elif mode == 'diff':
        lam4, gd = extra
        in_specs += [pl.BlockSpec((4, DC), lambda b, s, pt: (0, 0)), pl.BlockSpec((1, 256), lambda b, s, pt: (0, 0))]
        args += [lam4, gd]
    else:
        (sel,) = extra
        in_specs += [pl.BlockSpec((None, n_pages + 1, seq, LANES), lambda b, s, pt: (b, 0, 0, 0))]
        args += [sel]
    return pl.pallas_call(
        functools.partial(_attn_sample_kernel, mode=mode, group=group, n_pages=n_pages, seq=seq, lam_init=lam_init),
        grid_spec=pltpu.PrefetchScalarGridSpec(
            num_scalar_prefetch=1, grid=(batch, n_steps + 1), in_specs=in_specs,
            out_specs=pl.BlockSpec((None, seq, 256), lambda b, s, pt: (b, 0, 0)), scratch_shapes=scratch),
        out_shape=jax.ShapeDtypeStruct((batch, seq, 256), F32),
        compiler_params=_cparams(("parallel", "arbitrary")), name=mode + "_sample",
    )(page_table, *args)


def _dsa_select_sample_kernel(*refs, group, n_pages, seq, k_sel):
    refs = refs[1:]
    qi_ref, kin_ref, combn_ref = refs[:3]
    ik_refs = refs[3:3 + group]
    sk_ref, qix_ref = refs[3 + group:]
    s_idx = pl.program_id(1)
    n_steps = n_pages // group

    @pl.when(s_idx == 0)
    def _():
        qi = qi_ref[...]
        for h in range(N_HEADS):
            qix_ref[h * seq:(h + 1) * seq, :] = _masked_pair(qi, h, DI)[0]

    def scores(logits):
        comb = combn_ref[...]
        sc = jnp.zeros((seq, LANES), F32)
        for h in range(N_HEADS):
            wcol = comb[:, N_HEADS + h:N_HEADS + h + 1] * (DI ** -0.5 * N_HEADS ** -0.5)
            sc = sc + jnp.maximum(logits[h * seq:(h + 1) * seq, :], 0.0) * wcol
        return sc

    @pl.when(s_idx < n_steps)
    def _():
        qix = qix_ref[...].astype(BF16)
        for g in range(group):
            ikT = ik_refs[g][...].astype(BF16)
            sk_ref[s_idx * group + g] = _order_key(scores(_dot(qix, jnp.concatenate([ikT, ikT], axis=0))))

    @pl.when(s_idx == n_steps)
    def _():
        qix = qix_ref[...].astype(BF16)
        kin = kin_ref[...]
        kin2 = _pad_rows(kin + pltpu.roll(kin, DI, axis=1), LANES).astype(BF16)
        sc = scores(_dot_nt(qix, kin2))
        qrow = lax.broadcasted_iota(I32, (seq, LANES), 0)
        kcol = lax.broadcasted_iota(I32, (seq, LANES), 1)
        sk_ref[n_pages] = _order_key(jnp.where(kcol <= qrow, sc, -jnp.inf))


def _dsa_pick_sample_kernel(sk_ref, bias_ref, *, k_sel):
    nb, n_blk, seq, _ = sk_ref.shape

    def count_ge(cand):
        ind = jnp.where(sk_ref[...] >= cand, 1.0, 0.0)
        return jnp.sum(jnp.sum(ind, axis=1, keepdims=True), axis=3, keepdims=True)

    thr = _kth_largest_key(count_ge, float(k_sel), (nb, 1, seq, 1))
    keys = sk_ref[...]
    n_gt = jnp.sum(jnp.sum(jnp.where(keys > thr, 1.0, 0.0), axis=1, keepdims=True), axis=3, keepdims=True)
    need = (float(k_sel) - n_gt)[:, 0]
    eq = keys == thr
    eq2d = jnp.where(eq, 1.0, 0.0).reshape(nb * n_blk * seq, LANES).astype(BF16)
    within = _dot(eq2d, _upper_tri(LANES)).reshape(nb, n_blk, seq, LANES)
    seen = jnp.zeros((nb, seq, 1), F32)
    for j in range(n_blk):
        rank = seen + within[:, j]
        sel = (keys[:, j] > thr[:, 0]) | (eq[:, j] & (rank <= need))
        bias_ref[:, j] = jnp.where(sel, 0.0, NEG)
        seen = rank[:, :, LANES - 1:LANES]


def _dsa_select_sample(page_table, qi, ki_new, comb_new, cache_ikT, *, layer, batch, seq, group):
    n_pages = page_table.shape[1]
    n_steps = n_pages // group
    per_b = lambda width: pl.BlockSpec((None, seq, width), lambda b, s, pt: (b, 0, 0))
    in_specs = [per_b(256), per_b(LANES), per_b(LANES)]
    in_specs += [_page_spec(DI, layer, g, group, n_pages, False) for g in range(group)]
    k_sel = min(TOPK_MAX, (n_pages * PAGE_SIZE + seq) // 4)
    keys = pl.pallas_call(
        functools.partial(_dsa_select_sample_kernel, group=group, n_pages=n_pages, seq=seq, k_sel=k_sel),
        grid_spec=pltpu.PrefetchScalarGridSpec(
            num_scalar_prefetch=1, grid=(batch, n_steps + 1), in_specs=in_specs,
            out_specs=pl.BlockSpec((None, n_pages + 1, seq, LANES), lambda b, s, pt: (b, 0, 0, 0)),
            scratch_shapes=[pltpu.VMEM((N_HEADS * seq, LANES), F32)]),
        out_shape=jax.ShapeDtypeStruct((batch, n_pages + 1, seq, LANES), I32),
        compiler_params=_cparams(("parallel", "arbitrary")), name="dsa_score_sample",
    )(page_table, qi, ki_new, comb_new, *([cache_ikT] * group))
    shape = (batch, n_pages + 1, seq, LANES)
    whole = pl.BlockSpec(shape, lambda i: (0, 0, 0, 0))
    return pl.pallas_call(
        functools.partial(_dsa_pick_sample_kernel, k_sel=k_sel), grid=(1,), in_specs=[whole], out_specs=whole,
        out_shape=jax.ShapeDtypeStruct(shape, F32), compiler_params=_cparams(("arbitrary",)), name="dsa_pick_sample",
    )(keys)


def _mix_kernel(x_ref, oa_ref, ob_ref, oc_ref, od_ref, gpre_ref, wg_ref, bg_ref, wpa_ref, wpb_ref, wpc_ref,
                wpd_ref, wo_ref, gpost_ref, y_ref):
    x = x_ref[...]
    h = _rms(x, gpre_ref[...]).astype(BF16)
    mixed = None
    for n, (o_ref, wp_ref) in enumerate(((oa_ref, wpa_ref), (ob_ref, wpb_ref), (oc_ref, wpc_ref), (od_ref, wpd_ref))):
        gate = jax.nn.sigmoid(_dot(h, wg_ref[:, n * D_MODEL:(n + 1) * D_MODEL]) + bg_ref[:, n * D_MODEL:(n + 1) * D_MODEL])
        t = gate * _dot(o_ref[...].astype(BF16), wp_ref[...])
        mixed = t if mixed is None else mixed + t
    y = _dot(mixed.astype(BF16), wo_ref[...])
    y_ref[...] = x + _rms(y, gpost_ref[...])


def _mix(x2d, oa, ob, oc, od, gpre, wg, bg, wpa, wpb, wpc, wpd, wo, gpost, *, tm):
    n = x2d.shape[0]
    tile = lambda width: pl.BlockSpec((tm, width), lambda i: (i, 0))
    const = lambda a: pl.BlockSpec(a.shape, lambda i: (0, 0))
    consts = (gpre, wg, bg, wpa, wpb, wpc, wpd, wo, gpost)
    return pl.pallas_call(
        _mix_kernel, grid=(n // tm,),
        in_specs=[tile(D_MODEL), tile(512), tile(256), tile(256), tile(256)] + [const(a) for a in consts],
        out_specs=tile(D_MODEL), out_shape=jax.ShapeDtypeStruct((n, D_MODEL), F32),
        compiler_params=_cparams(("parallel",)), name="mix",
    )(x2d, oa, ob, oc, od, *consts)


def _ffn_kernel(x_ref, gpre_ref, wgu_ref, wd_ref, gpost_ref, y_ref):
    x = x_ref[...]
    h = _rms(x, gpre_ref[...]).astype(BF16)
    g = _dot(h, wgu_ref[:, :D_FF])
    u = _dot(h, wgu_ref[:, D_FF:])
    a = (g * jax.nn.sigmoid(g) * u).astype(BF16)
    y_ref[...] = x + _rms(_dot(a, wd_ref[...]), gpost_ref[...])


def _ffn(x2d, gpre, wgu, wd, gpost, *, tm):
    n = x2d.shape[0]
    tile = pl.BlockSpec((tm, D_MODEL), lambda i: (i, 0))
    const = lambda a: pl.BlockSpec(a.shape, lambda i: (0, 0))
    consts = (gpre, wgu, wd, gpost)
    return pl.pallas_call(
        _ffn_kernel, grid=(n // tm,), in_specs=[tile] + [const(a) for a in consts],
        out_specs=tile, out_shape=jax.ShapeDtypeStruct((n, D_MODEL), F32),
        compiler_params=_cparams(("parallel",)), name="ffn",
    )(x2d, *consts)


def _rope_tables(pos):
    def one(headw, rot, theta):
        half = rot // 2
        inv = theta ** (-jnp.arange(half, dtype=F32) / half)
        ang = pos.astype(F32)[:, None] * inv[None, :]
        cos, sin = jnp.cos(ang), jnp.sin(ang)
        rest = headw - rot
        n = pos.shape[0]
        c = jnp.concatenate([cos, cos, jnp.ones((n, rest), F32)], axis=1)
        s = jnp.concatenate([-sin, sin, jnp.zeros((n, rest), F32)], axis=1)
        return jnp.tile(c, (1, 256 // headw)), jnp.tile(s, (1, 256 // headw))
    cr, sr = one(DKA, DKA, RET_THETA)
    c32, s32 = one(DC, DC // ROPE_FRAC, ROPE_THETA)
    c64, s64 = one(DH, DH // ROPE_FRAC, ROPE_THETA)
    return (cr, sr, c32, s32, c64, s64)


def _pack_w_in(w_in):
    depth = w_in.shape[0]
    seg = lambda name: w_in[:, :, _SRC[name][0]:_SRC[name][0] + _SRC[name][1]]
    zeros = lambda width: jnp.zeros((depth, D_MODEL, width), w_in.dtype)
    parts = [seg(n) for n in _PACK_ORDER]
    parts += [seg('ki'), zeros(LANES - DI), seg('fb'), seg('wi'), zeros(LANES - 2 * N_HEADS)]
    return jnp.concatenate(parts, axis=2).astype(BF16)


def _heads_T(a, batch, seq):
    return a.reshape(batch, N_HEADS, a.shape[1] // N_HEADS, seq).transpose(0, 3, 1, 2)


def kernel(x_prompt, x_sample, state_ret, cache_fox_k, cache_fox_v, cache_fox_lf, cache_diff_k, cache_diff_v, cache_dsa_k, cache_dsa_v, cache_dsa_ik, page_table, g_pre_mix, g_post_mix, g_pre_ffn, g_post_ffn, w_in, b_forget, lam_q1, lam_k1, lam_q2, lam_k2, g_diff, w_pa, w_pb, w_pc, w_pd, w_gate, b_gate, w_out, w_gu, w_down):
    bp, tp, _ = x_prompt.shape
    bs, ts, _ = x_sample.shape
    depth = w_in.shape[0]
    n_pool = cache_fox_k.shape[1]
    past = page_table.shape[1] * PAGE_SIZE
    tm_p = min(256, tp)
    tb = min(256, tp)
    group = min(16, page_table.shape[1])

    w_pack = _pack_w_in(w_in)
    bf_pad = jnp.pad(b_forget, ((0, 0), (0, LANES - N_HEADS)))[:, None, :]
    tabs_p = _rope_tables(jnp.arange(tp, dtype=I32))
    tabs_s = _rope_tables(jnp.tile(past + jnp.arange(ts, dtype=I32), bs))
    row = lambda a, l: a[l][None, :]
    bf = lambda a: a.astype(BF16)
    wg16, wpa16, wpb16, wpc16, wpd16, wo16, wgu16, wd16 = map(bf, (w_gate, w_pa, w_pb, w_pc, w_pd, w_out, w_gu, w_down))
    lam4 = jnp.stack([lam_q1, lam_k1, lam_q2, lam_k2], axis=1)
    pageT = lambda c: jnp.moveaxis(c.reshape(depth, n_pool, PAGE_SIZE, -1), 2, 3)
    fkT, fvT, ckT, cvT, dkT, dvT, ikT, lfT = map(pageT, (cache_fox_k, cache_fox_v, cache_diff_k, cache_diff_v,
                                                         cache_dsa_k, cache_dsa_v, cache_dsa_ik, cache_fox_lf))

    xp = x_prompt.reshape(bp * tp, D_MODEL)
    xs = x_sample.reshape(bs * ts, D_MODEL)
    new = {n: [] for n in ('ret_p', 'ret_s', 'fk_p', 'fv_p', 'flf_p', 'fk_s', 'fv_s', 'flf_s', 'ck_p', 'cv_p',
                           'ck_s', 'cv_s', 'dk_p', 'dv_p', 'dik_p', 'dk_s', 'dv_s', 'dik_s')}
    for l in range(depth):
        lam_init = 0.8 - 0.6 * math.exp(-0.3 * l)
        finish = lambda x, oa, ob, oc, od, tm: _ffn(
            _mix(x, oa, ob, oc, od, row(g_pre_mix, l), wg16[l], row(b_gate, l), wpa16[l], wpb16[l], wpc16[l],
                 wpd16[l], wo16[l], row(g_post_mix, l), tm=tm),
            row(g_pre_ffn, l), wgu16[l], wd16[l], row(g_post_ffn, l), tm=tm)
        pr = _proj(xp, row(g_pre_mix, l), w_pack[l], bf_pad[l], tabs_p, mode='prompt', batch=bp, seq=tp, tm=tm_p)
        oa, r_p = _ret_prompt(pr['qa'], pr['ka'], pr['va'], pr['ga'], batch=bp, seq=tp)
        ob = _attn_prompt('fox', pr['qb'], pr['kb'], pr['vb'], (pr['comb'], pr['auxT']), batch=bp, seq=tp, tb=tb)
        oc = _attn_prompt('diff', pr['qc'], pr['kc'], pr['vc'], (lam4[l], row(g_diff, l)), batch=bp, seq=tp, tb=tb,
                          lam_init=lam_init)
        od = _attn_prompt('dsa', pr['qd'], pr['kd'], pr['vd'], (pr['qi'], pr['ki'], pr['comb']), batch=bp, seq=tp, tb=tb)
        xp = finish(xp, oa, ob, oc, od, tm_p)
        new['ret_p'].append(r_p.reshape(bp, N_HEADS, DKA, DVA))
        for name, key in (('fk_p', 'kb'), ('fv_p', 'vb'), ('ck_p', 'kc'), ('cv_p', 'vc'), ('dk_p', 'kd'), ('dv_p', 'vd')):
            new[name].append(_heads_T(pr[key], bp, tp))
        new['flf_p'].append(jnp.swapaxes(pr['auxT'][:, :N_HEADS, :], 1, 2))
        new['dik_p'].append(jnp.swapaxes(pr['ki'], 1, 2))
        sm = _proj(xs, row(g_pre_mix, l), w_pack[l], bf_pad[l], tabs_s, mode='sample', batch=bs, seq=ts, tm=bs * ts)
        oa, r_s = _ret_sample(sm['qa'], sm['ka'], sm['va'], sm['ga'], state_ret[l].reshape(bs, 256, DVA), batch=bs, seq=ts)
        per_b = lambda a: a.reshape(bs, ts, a.shape[-1])
        common = dict(layer=l, batch=bs, seq=ts, group=group)
        ob = _attn_sample('fox', page_table, per_b(sm['qb']), per_b(sm['kb']), per_b(sm['vb']), fkT, fvT,
                          (lfT, per_b(sm['comb'])), **common)
        oc = _attn_sample('diff', page_table, per_b(sm['qc']), per_b(sm['kc']), per_b(sm['vc']), ckT, cvT,
                          (lam4[l], row(g_diff, l)), lam_init=lam_init, **common)
        sel = _dsa_select_sample(page_table, per_b(sm['qi']), per_b(sm['ki']), per_b(sm['comb']), ikT, **common)
        od = _attn_sample('dsa', page_table, per_b(sm['qd']), per_b(sm['kd']), per_b(sm['vd']), dkT, dvT, (sel,), **common)
        flat = lambda a: a.reshape(bs * ts, 256)
        xs = finish(xs, oa, flat(ob), flat(oc), flat(od), bs * ts)
        new['ret_s'].append(r_s.reshape(bs, N_HEADS, DKA, DVA))
        for name, key in (('fk_s', 'kb'), ('fv_s', 'vb'), ('ck_s', 'kc'), ('cv_s', 'vc'), ('dk_s', 'kd'), ('dv_s', 'vd')):
            new[name].append(sm[key].reshape(bs, ts, N_HEADS, DH))
        new['flf_s'].append(sm['comb'][:, :N_HEADS].reshape(bs, ts, N_HEADS))
        new['dik_s'].append(sm['ki'][:, :DI].reshape(bs, ts, DI))
    st = lambda n: jnp.stack(new[n], axis=0)
    return (xp.reshape(bp, tp, D_MODEL), xs.reshape(bs, ts, D_MODEL), st('ret_p'), st('ret_s'),
            st('fk_p'), st('fv_p'), st('flf_p'), st('fk_s'), st('fv_s'), st('flf_s'),
            st('ck_p'), st('cv_p'), st('ck_s'), st('cv_s'),
            st('dk_p'), st('dv_p'), st('dik_p'), st('dk_s'), st('dv_s'), st('dik_s'))
```

```python
import functools
import math

import jax
import jax.numpy as jnp
from jax import lax
from jax.experimental import pallas as pl
from jax.experimental.pallas import tpu as pltpu

F32 = jnp.float32
BF16 = jnp.bfloat16
I32 = jnp.int32

D_MODEL = 1024
HEAD_DIM = 64
N_HEADS = 4
DKA = HEAD_DIM
DVA = 2 * HEAD_DIM
RET_CHUNK = 128
RET_THETA = 10000.0
DH = HEAD_DIM
DC = HEAD_DIM // 2
DI = HEAD_DIM
TOPK_MAX = 256
ROPE_THETA = 500000.0
ROPE_FRAC = 4
D_FF = 2816
EPS = 1e-6
PAGE_SIZE = 128
LANES = 128
NEG = -1e30
INT_MIN = -2 ** 31
LOG2E = math.log2(math.e)
VMEM_LIMIT = 56 * 1024 * 1024

LOG_GAMMA = tuple(math.log1p(-(2.0 ** (-5.0 - h))) for h in range(N_HEADS))

_SRC = dict(qa=(0, 256), ka=(256, 256), va=(512, 512), ga=(1024, 512),
            qb=(1536, 256), kb=(1792, 256), vb=(2048, 256), fb=(2304, 4),
            qc=(2308, 256), kc=(2564, 256), vc=(2820, 256),
            qd=(3076, 256), kd=(3332, 256), vd=(3588, 256),
            qi=(3844, 256), ki=(4100, 64), wi=(4164, 4))
_PACK_ORDER = ('qa', 'ka', 'va', 'ga', 'qb', 'kb', 'vb', 'qc', 'kc', 'vc', 'qd', 'kd', 'vd', 'qi')
_GRP = {}
_off = 0
for _n in _PACK_ORDER:
    _GRP[_n] = (_off, _SRC[_n][1])
    _off += _SRC[_n][1]
_GRP['ki'] = (_off, LANES)
_off += LANES
_GRP['aux'] = (_off, LANES)
_off += LANES
N_PACK = _off


def _cparams(sem):
    return pltpu.CompilerParams(dimension_semantics=sem, vmem_limit_bytes=VMEM_LIMIT)


def _rms(x, g=None):
    y = x * lax.rsqrt(jnp.mean(x * x, axis=-1, keepdims=True) + EPS)
    return y if g is None else y * g


def _dot(a, b):
    return jnp.dot(a, b, preferred_element_type=F32)


def _dot_nt(a, b):
    return lax.dot_general(a, b, (((1,), (1,)), ((), ())), preferred_element_type=F32)


def _split3(x):
    x1 = x.astype(BF16)
    r1 = x - x1.astype(F32)
    x2 = r1.astype(BF16)
    x3 = (r1 - x2.astype(F32)).astype(BF16)
    return x1, x2, x3


def _lane_mask(width, lo, hi):
    lane = lax.broadcasted_iota(I32, (1, width), 1)
    return (lane >= lo) & (lane < hi)


def _masked_pair(q, g, w):
    p = (g * w) // LANES
    lo = g * w - p * LANES
    qp = q[:, p * LANES:(p + 1) * LANES]
    return jnp.where(_lane_mask(LANES, lo, lo + w), qp, jnp.zeros_like(qp)), p


def _order_key(score):
    bits = lax.bitcast_convert_type(score + 0.0, I32)
    return jnp.where(bits < 0, bits ^ jnp.int32(0x7FFFFFFF), bits)


def _kth_largest_key(count_ge, k, shape):
    nonneg = count_ge(jnp.zeros(shape, I32)) >= k
    lo0 = jnp.where(nonneg, jnp.int32(0), jnp.int32(INT_MIN))

    def body(b, lo):
        cand = lo + jnp.left_shift(jnp.int32(1), 30 - b)
        return jnp.where(count_ge(cand) >= k, cand, lo)

    return lax.fori_loop(0, 31, body, lo0)


_PROMPT_OUTS = (('qa', 'tok', BF16), ('ka', 'T', BF16), ('va', 'tok', BF16), ('ga', 'tok', F32),
                ('qb', 'tok', BF16), ('kb', 'T', F32), ('vb', 'T', F32),
                ('qc', 'tok', BF16), ('kc', 'T', F32), ('vc', 'T', F32),
                ('qd', 'tok', BF16), ('kd', 'T', F32), ('vd', 'T', F32),
                ('qi', 'tok', BF16), ('ki', 'T', F32), ('comb', 'tok', F32), ('auxT', 'T', F32))
_SAMPLE_OUTS = (('qa', 'tok', F32), ('ka', 'T', F32), ('va', 'tok', F32), ('ga', 'tok', F32),
                ('qb', 'tok', F32), ('kb', 'tok', F32), ('vb', 'tok', F32),
                ('qc', 'tok', F32), ('kc', 'tok', F32), ('vc', 'tok', F32),
                ('qd', 'tok', F32), ('kd', 'tok', F32), ('vd', 'tok', F32),
                ('qi', 'tok', F32), ('ki', 'tok', F32), ('comb', 'tok', F32))
_ROPE_KIND = dict(qa='ret', ka='ret', qc='r32', kc='r32', qd='r64', kd='r64', qi='r64', ki='r64')
_SCALE = dict(ka=DKA ** -0.5, qb=DH ** -0.5 * LOG2E, qc=DC ** -0.5 * LOG2E, qd=DH ** -0.5 * LOG2E)
_T_ROWS = dict(ki=DI, auxT=16)


def _proj_kernel(*refs, outs, with_cumsum, tiles_per_seq):
    (x_ref, g_ref, w_ref, bf_ref, cr_ref, sr_ref, c32_ref, s32_ref, c64_ref, s64_ref) = refs[:10]
    out_refs = dict(zip([o[0] for o in outs], refs[10:10 + len(outs)]))
    tm = x_ref.shape[0]
    h = _rms(x_ref[...], g_ref[...]).astype(BF16)
    tables = dict(ret=(cr_ref, sr_ref, DKA, DKA // 2),
                  r32=(c32_ref, s32_ref, DC, DC // ROPE_FRAC // 2),
                  r64=(c64_ref, s64_ref, DH, DH // ROPE_FRAC // 2))

    def group(name):
        start, width = _GRP[name]
        p = _dot(h, w_ref[:, start:start + width])
        kind = _ROPE_KIND.get(name)
        if kind is not None:
            c_ref, s_ref, headw, half = tables[kind]
            lane = lax.broadcasted_iota(I32, (1, width), 1)
            first = (lane & (headw - 1)) < half
            rx = jnp.where(first, pltpu.roll(p, width - half, axis=1), pltpu.roll(p, half, axis=1))
            p = p * c_ref[:, :width] + rx * s_ref[:, :width]
        if name in _SCALE:
            p = p * _SCALE[name]
        return p

    comb = None
    for name, layout, dtype in outs:
        ref = out_refs[name]
        if name in ('comb', 'auxT'):
            if comb is None:
                a = group('aux')
                z = a + bf_ref[...]
                lf = jnp.minimum(z, 0.0) - jnp.log1p(jnp.exp(-jnp.abs(z)))
                lane = lax.broadcasted_iota(I32, (1, LANES), 1)
                comb = jnp.where(lane < N_HEADS, lf, a)
                if with_cumsum:
                    carry_ref = refs[-1]

                    @pl.when(pl.program_id(0) % tiles_per_seq == 0)
                    def _():
                        carry_ref[...] = jnp.zeros_like(carry_ref)

                    r = lax.broadcasted_iota(I32, (tm, tm), 0)
                    c = lax.broadcasted_iota(I32, (tm, tm), 1)
                    tri = jnp.where(c <= r, 1.0, 0.0).astype(BF16)
                    l1, l2, l3 = _split3(lf)
                    cs = _dot(tri, l1) + _dot(tri, l2) + _dot(tri, l3) + carry_ref[...]
                    carry_ref[...] = cs[tm - 1:tm, :]
                    comb = jnp.where(lane < 2 * N_HEADS, comb,
                                     jnp.where(lane < 3 * N_HEADS, pltpu.roll(cs, 2 * N_HEADS, axis=1), 0.0))
            val = comb
        else:
            val = group(name)
        if layout == 'tok':
            ref[...] = val.astype(dtype)
        else:
            vt = val.T
            rows = _T_ROWS.get(name, vt.shape[0])
            ref[...] = vt[:rows, :].astype(dtype)


def _proj(x2d, gain, w_pack, bf_pad, tabs, *, mode, batch, seq, tm):
    n = x2d.shape[0]
    outs = _PROMPT_OUTS if mode == 'prompt' else _SAMPLE_OUTS
    tps = seq // tm if mode == 'prompt' else 1
    n_tiles = n // tm
    tab_rows = tabs[0].shape[0]
    tab_tiles = tab_rows // tm
    in_specs = [pl.BlockSpec((tm, D_MODEL), lambda i: (i, 0)),
                pl.BlockSpec((1, D_MODEL), lambda i: (0, 0)),
                pl.BlockSpec((D_MODEL, N_PACK), lambda i: (0, 0)),
                pl.BlockSpec((1, LANES), lambda i: (0, 0))]
    in_specs += [pl.BlockSpec((tm, 256), lambda i: (i % tab_tiles, 0)) for _ in range(6)]
    out_specs, out_shapes = [], []
    for name, layout, dtype in outs:
        width = LANES if name in ('comb', 'auxT', 'ki') else _GRP[name][1]
        if layout == 'tok':
            out_specs.append(pl.BlockSpec((tm, width), lambda i: (i, 0)))
            out_shapes.append(jax.ShapeDtypeStruct((n, width), dtype))
        else:
            rows = _T_ROWS.get(name, width)
            if mode == 'prompt':
                out_specs.append(pl.BlockSpec((None, rows, tm), lambda i: (i // tps, 0, i % tps)))
                out_shapes.append(jax.ShapeDtypeStruct((batch, rows, seq), dtype))
            else:
                out_specs.append(pl.BlockSpec((rows, tm), lambda i: (0, i)))
                out_shapes.append(jax.ShapeDtypeStruct((rows, n), dtype))
    res = pl.pallas_call(
        functools.partial(_proj_kernel, outs=outs, with_cumsum=(mode == 'prompt'), tiles_per_seq=tps),
        grid=(n_tiles,), in_specs=in_specs, out_specs=out_specs, out_shape=out_shapes,
        scratch_shapes=[pltpu.VMEM((1, LANES), F32)],
        compiler_params=_cparams(("arbitrary",)), name="proj_" + mode,
    )(x2d, gain, w_pack, bf_pad, *tabs)
    return dict(zip([o[0] for o in outs], res))


def _ret_finish(o, ga_h):
    return _rms(o) * (ga_h * jax.nn.sigmoid(ga_h))


def _ret_prompt_kernel(q_ref, kT_ref, v_ref, ga_ref, o_ref, st_ref, r_ref):
    c = q_ref.shape[0]
    j = pl.program_id(1)

    @pl.when(j == 0)
    def _():
        r_ref[...] = jnp.zeros_like(r_ref)

    q = q_ref[...]
    row = lax.broadcasted_iota(I32, (c, c), 0)
    col = lax.broadcasted_iota(I32, (c, c), 1)
    rel = (row - col).astype(F32)
    irow = lax.broadcasted_iota(I32, (c, 1), 0).astype(F32)
    icol = lax.broadcasted_iota(I32, (1, c), 1).astype(F32)
    for h in range(N_HEADS):
        lg = LOG_GAMMA[h]
        dmask = jnp.where(rel >= 0, jnp.exp(jnp.maximum(rel, 0.0) * lg), 0.0)
        xi = jnp.exp((irow + 1.0) * lg)
        zeta = jnp.exp((c - 1.0 - icol) * lg)
        qp, p = _masked_pair(q, h, DKA)
        s = _dot(qp, kT_ref[p * LANES:(p + 1) * LANES, :]) * dmask
        v_h = v_ref[:, h * DVA:(h + 1) * DVA]
        r_pair = r_ref[p * LANES:(p + 1) * LANES, :].astype(BF16)
        o = _dot(s.astype(BF16), v_h) + _dot((qp.astype(F32) * xi).astype(BF16), r_pair)
        kz = (kT_ref[h * DKA:(h + 1) * DKA, :].astype(F32) * zeta).astype(BF16)
        r_ref[h * DKA:(h + 1) * DKA, :] = math.exp(c * lg) * r_ref[h * DKA:(h + 1) * DKA, :] + _dot(kz, v_h)
        o_ref[:, h * DVA:(h + 1) * DVA] = _ret_finish(o, ga_ref[:, h * DVA:(h + 1) * DVA]).astype(o_ref.dtype)

    @pl.when(j == pl.num_programs(1) - 1)
    def _():
        st_ref[...] = r_ref[...]


def _ret_prompt(qa, kaT, va, ga, *, batch, seq):
    c = RET_CHUNK
    nc = seq // c
    return pl.pallas_call(
        _ret_prompt_kernel, grid=(batch, nc),
        in_specs=[pl.BlockSpec((c, 256), lambda b, j: (b * nc + j, 0)),
                  pl.BlockSpec((None, 256, c), lambda b, j: (b, 0, j)),
                  pl.BlockSpec((c, 512), lambda b, j: (b * nc + j, 0)),
                  pl.BlockSpec((c, 512), lambda b, j: (b * nc + j, 0))],
        out_specs=[pl.BlockSpec((c, 512), lambda b, j: (b * nc + j, 0)),
                   pl.BlockSpec((None, 256, DVA), lambda b, j: (b, 0, 0))],
        out_shape=[jax.ShapeDtypeStruct((batch * seq, 512), BF16),
                   jax.ShapeDtypeStruct((batch, 256, DVA), F32)],
        scratch_shapes=[pltpu.VMEM((256, DVA), F32)],
        compiler_params=_cparams(("parallel", "arbitrary")), name="ret_prompt",
    )(qa, kaT, va, ga)


def _ret_sample_kernel(q_ref, kT_ref, v_ref, ga_ref, st_ref, o_ref, stn_ref, *, batch, seq):
    n = batch * seq
    q = q_ref[...]
    row = lax.broadcasted_iota(I32, (n, n), 0)
    col = lax.broadcasted_iota(I32, (n, n), 1)
    same = (row // seq) == (col // seq)
    rel = ((row % seq) - (col % seq)).astype(F32)
    irow = (lax.broadcasted_iota(I32, (n, 1), 0) % seq).astype(F32)
    icol_i = lax.broadcasted_iota(I32, (1, n), 1)
    icol = (icol_i % seq).astype(F32)
    seq_lane = lax.broadcasted_iota(I32, (1, batch * LANES), 1) // LANES
    tok_row = lax.broadcasted_iota(I32, (n, 1), 0) // seq
    own_state = seq_lane == tok_row
    seq_sub = lax.broadcasted_iota(I32, (batch * DKA, 1), 0) // DKA
    own_tok = seq_sub == (icol_i // seq)
    for h in range(N_HEADS):
        lg = LOG_GAMMA[h]
        dmask = jnp.where(same & (rel >= 0), jnp.exp(jnp.maximum(rel, 0.0) * lg), 0.0)
        xi = jnp.exp((irow + 1.0) * lg)
        zeta = jnp.exp((seq - 1.0 - icol) * lg)
        qp, p = _masked_pair(q, h, DKA)
        kT_pair = kT_ref[p * LANES:(p + 1) * LANES, :].astype(BF16)
        s = _dot(qp.astype(BF16), kT_pair) * dmask
        v_h = v_ref[:, h * DVA:(h + 1) * DVA].astype(BF16)
        qx = (qp * xi).astype(BF16)
        q_big = jnp.where(own_state, jnp.concatenate([qx] * batch, axis=1), jnp.zeros((), BF16))
        r_pair = st_ref[:, p * LANES:(p + 1) * LANES, :].reshape(batch * LANES, DVA).astype(BF16)
        o = _dot(s.astype(BF16), v_h) + _dot(q_big, r_pair)
        o_ref[:, h * DVA:(h + 1) * DVA] = _ret_finish(o, ga_ref[:, h * DVA:(h + 1) * DVA]).astype(o_ref.dtype)
        kz = (kT_ref[h * DKA:(h + 1) * DKA, :] * zeta).astype(BF16)
        k_big = jnp.where(own_tok, jnp.concatenate([kz] * batch, axis=0), jnp.zeros((), BF16))
        upd = _dot(k_big, v_h).reshape(batch, DKA, DVA)
        stn_ref[:, h * DKA:(h + 1) * DKA, :] = math.exp(seq * lg) * st_ref[:, h * DKA:(h + 1) * DKA, :] + upd


def _ret_sample(qa, kaT, va, ga, state, *, batch, seq):
    n = batch * seq
    full = lambda shape: pl.BlockSpec(shape, lambda i: (0,) * len(shape))
    return pl.pallas_call(
        functools.partial(_ret_sample_kernel, batch=batch, seq=seq), grid=(1,),
        in_specs=[full((n, 256)), full((256, n)), full((n, 512)), full((n, 512)), full((batch, 256, DVA))],
        out_specs=[full((n, 512)), full((batch, 256, DVA))],
        out_shape=[jax.ShapeDtypeStruct((n, 512), BF16), jax.ShapeDtypeStruct((batch, 256, DVA), F32)],
        compiler_params=_cparams(("arbitrary",)), name="ret_sample",
    )(qa, kaT, va, ga, state)


def _lam_value(lam_ref, lam_init):
    l4 = lam_ref[...]
    s1 = jnp.sum(l4[0:1, :] * l4[1:2, :], axis=1, keepdims=True)
    s2 = jnp.sum(l4[2:3, :] * l4[3:4, :], axis=1, keepdims=True)
    return jnp.exp(s1) - jnp.exp(s2) + lam_init


def _online(s, mask, m, l):
    if mask is not None:
        s = jnp.where(mask, s, NEG)
    m2 = jnp.maximum(m, jnp.max(s, axis=1, keepdims=True))
    alpha = jnp.exp2(m - m2)
    p = jnp.exp2(s - m2)
    if mask is not None:
        p = jnp.where(mask, p, 0.0)
    return p, m2, alpha, l * alpha + jnp.sum(p, axis=1, keepdims=True)


def _upper_tri(n):
    r = lax.broadcasted_iota(I32, (n, n), 0)
    c = lax.broadcasted_iota(I32, (n, n), 1)
    return jnp.where(r <= c, 1.0, 0.0).astype(BF16)


def _attn_prompt_kernel(*refs, mode, tb, tk, k_sel, lam_init):
    i = pl.program_id(1)
    n_full = (i * tb) // tk
    m_ref, acc_ref, p_ref = refs[-3:]
    refs = refs[:-3]
    if mode == 'fox':
        q_ref, kT_ref, vT_ref, comb_ref, auxT_ref, o_ref, cq_ref = refs
    elif mode == 'diff':
        q_ref, kT_ref, vT_ref, lam_ref, gd_ref, o_ref = refs
    else:
        q_ref, kT_ref, vT_ref, qi_ref, kiT_ref, comb_ref, o_ref, sk_ref = refs
    q = q_ref[...]
    row = lax.broadcasted_iota(I32, (tb, tk), 0)
    col = lax.broadcasted_iota(I32, (tb, tk), 1)
    causal = (n_full * tk + col) <= (i * tb + row)

    def blk(j):
        return pl.ds(pl.multiple_of(j * tk, tk), tk)

    if mode == 'dsa':
        qi = qi_ref[...]
        comb = comb_ref[...]
        qips = [_masked_pair(qi, h, DI)[0] for h in range(N_HEADS)]
        wcols = [comb[:, N_HEADS + h:N_HEADS + h + 1] * (DI ** -0.5 * N_HEADS ** -0.5) for h in range(N_HEADS)]

        def score_block(j, diag):
            kiT = kiT_ref[:, blk(j)].astype(BF16)
            kiT2 = jnp.concatenate([kiT, kiT], axis=0)
            sc = jnp.zeros((tb, tk), F32)
            for h in range(N_HEADS):
                sc = sc + jnp.maximum(_dot(qips[h], kiT2), 0.0) * wcols[h]
            if diag:
                sc = jnp.where(causal, sc, -jnp.inf)
            sk_ref[j] = _order_key(sc)

        def score_body(j, carry):
            score_block(j, False)
            return carry

        lax.fori_loop(0, n_full, score_body, 0)
        score_block(n_full, True)

        def count_ge(cand):
            def body(j, acc):
                ind = jnp.where(sk_ref[j] >= cand, 1.0, 0.0)
                for c in range(tk // LANES):
                    acc = acc + ind[:, c * LANES:(c + 1) * LANES]
                return acc
            acc = lax.fori_loop(0, n_full + 1, body, jnp.zeros((tb, LANES), F32))
            return jnp.sum(acc, axis=1, keepdims=True)

        thr = _kth_largest_key(count_ge, float(k_sel), (tb, 1))

        def gt_body(j, acc):
            ind = jnp.where(sk_ref[j] > thr, 1.0, 0.0)
            for c in range(tk // LANES):
                acc = acc + ind[:, c * LANES:(c + 1) * LANES]
            return acc
        n_gt = jnp.sum(lax.fori_loop(0, n_full + 1, gt_body, jnp.zeros((tb, LANES), F32)), axis=1, keepdims=True)
        need = float(k_sel) - n_gt
        tw = min(tk, 256)
        tri = _upper_tri(tw)

        def sel_block(j, seen, diag):
            x = sk_ref[j]
            eq = x == thr
            eqf = jnp.where(eq, 1.0, 0.0).astype(BF16)
            ranks = []
            for c in range(tk // tw):
                ranks.append(seen + _dot(eqf[:, c * tw:(c + 1) * tw], tri))
                seen = ranks[-1][:, tw - 1:tw]
            rank = ranks[0] if len(ranks) == 1 else jnp.concatenate(ranks, axis=1)
            sel = (x > thr) | (eq & (rank <= need))
            if diag:
                sel = sel & causal
            sk_ref[j] = lax.bitcast_convert_type(jnp.where(sel, 0.0, NEG), I32)
            return seen

        seen = lax.fori_loop(0, n_full, lambda j, s: sel_block(j, s, False), jnp.zeros((tb, 1), F32))
        sel_block(n_full, seen, True)

    n_maps = 2 * N_HEADS if mode == 'diff' else N_HEADS
    w = 256 // n_maps
    qps = [_masked_pair(q, g, w) for g in range(n_maps)]
    if mode == 'fox':
        for g in range(n_maps):
            cq_ref[g] = jnp.broadcast_to(comb_ref[:, 2 * N_HEADS + g:2 * N_HEADS + g + 1] * LOG2E, (tb, LANES))
    m_ref[...] = jnp.full_like(m_ref, NEG)
    acc_ref[...] = jnp.zeros_like(acc_ref)
    ones_row = jnp.where(lax.broadcasted_iota(I32, (LANES - DH, tk), 0) == 0, 1.0, 0.0).astype(BF16)
    n_sub = tk // LANES

    def values(j):
        return [jnp.concatenate([vT_ref[h * DH:(h + 1) * DH, blk(j)].astype(BF16), ones_row], axis=0)
                for h in range(N_HEADS)]

    p_ref[...] = jnp.zeros_like(p_ref)

    def step(j, diag):
        kTs = [kT_ref[p * LANES:(p + 1) * LANES, blk(j)].astype(BF16) for p in range(2)]
        vTs = values(jnp.maximum(j - 1, 0))
        if mode == 'dsa':
            bias = lax.bitcast_convert_type(sk_ref[j], F32)
        for g in range(n_maps):
            qp, p = qps[g]
            hv = g // 2 if mode == 'diff' else g
            pv = _dot_nt(p_ref[g], vTs[hv])
            s = _dot(qp, kTs[p])
            if mode == 'fox':
                s = s - auxT_ref[2 * N_HEADS + g:2 * N_HEADS + g + 1, blk(j)] * LOG2E
            if mode == 'dsa':
                s = s + bias
            elif diag:
                s = jnp.where(causal, s, NEG)
            parts = [s[:, c * LANES:(c + 1) * LANES] for c in range(n_sub)]
            top = parts[0]
            for c in range(1, n_sub):
                top = jnp.maximum(top, parts[c])
            top = jnp.broadcast_to(jnp.max(top, axis=1, keepdims=True), (tb, LANES))
            m_old = m_ref[g]
            if mode == 'fox':
                m_new = jnp.maximum(m_old, top + cq_ref[g])
                shift = m_new - cq_ref[g]
            else:
                m_new = jnp.maximum(m_old, top)
                shift = m_new
            m_ref[g] = m_new
            p_ref[g] = jnp.concatenate([jnp.exp2(part - shift).astype(BF16) for part in parts], axis=1)
            acc_ref[g] = (acc_ref[g] + pv) * jnp.exp2(m_old - m_new)

    def full_step(j, carry):
        step(j, False)
        return carry

    lax.fori_loop(0, n_full, full_step, 0)
    step(n_full, True)
    vTs = values(n_full)
    results = []
    for g in range(n_maps):
        acc = acc_ref[g] + _dot_nt(p_ref[g], vTs[g // 2 if mode == 'diff' else g])
        results.append(acc[:, :DH] / acc[:, DH:DH + 1])

    if mode == 'diff':
        lam = _lam_value(lam_ref, lam_init)
        for h in range(N_HEADS):
            o = results[2 * h] - lam * results[2 * h + 1]
            o = _rms(o, gd_ref[:, h * DH:(h + 1) * DH]) * (1.0 - lam_init)
            o_ref[:, h * DH:(h + 1) * DH] = o.astype(o_ref.dtype)
    else:
        for h in range(N_HEADS):
            o_ref[:, h * DH:(h + 1) * DH] = results[h].astype(o_ref.dtype)


def _attn_prompt(mode, q, kT, vT, extra, *, batch, seq, tb, tk, lam_init=0.0):
    nq = seq // tb
    tile = lambda width: pl.BlockSpec((tb, width), lambda b, i: (b * nq + i, 0))
    resident = lambda rows: pl.BlockSpec((None, rows, seq), lambda b, i: (b, 0, 0))
    in_specs = [tile(256), resident(256), resident(256)]
    scratch = []
    if mode == 'fox':
        in_specs += [tile(LANES), resident(16)]
    elif mode == 'diff':
        in_specs += [pl.BlockSpec((4, DC), lambda b, i: (0, 0)), pl.BlockSpec((1, 256), lambda b, i: (0, 0))]
    else:
        in_specs += [tile(256), resident(DI), tile(LANES)]
        scratch = [pltpu.VMEM((seq // tk, tb, tk), I32)]
    n_maps = 2 * N_HEADS if mode == 'diff' else N_HEADS
    if mode == 'fox':
        scratch = [pltpu.VMEM((n_maps, tb, LANES), F32)]
    scratch += [pltpu.VMEM((n_maps, tb, LANES), F32), pltpu.VMEM((n_maps, tb, LANES), F32),
                pltpu.VMEM((n_maps, tb, tk), BF16)]
    return pl.pallas_call(
        functools.partial(_attn_prompt_kernel, mode=mode, tb=tb, tk=tk, k_sel=min(TOPK_MAX, seq // 4),
                          lam_init=lam_init),
        grid=(batch, nq), in_specs=in_specs, out_specs=tile(256),
        out_shape=jax.ShapeDtypeStruct((batch * seq, 256), BF16), scratch_shapes=scratch,
        compiler_params=_cparams(("parallel", "arbitrary")), name=mode + "_prompt",
    )(q, kT, vT, *extra)


def _pad_rows(x, rows):
    return jnp.concatenate([x, jnp.zeros((rows - x.shape[0], x.shape[1]), x.dtype)], axis=0)


def _tile_rows(x, reps):
    return jnp.concatenate([x] * reps, axis=0)


def _prefix_rows(x):
    n = x.shape[0]
    rid = lax.broadcasted_iota(I32, (n, 1), 0)
    out = jnp.zeros_like(x)
    for u in range(n):
        out = out + jnp.where(rid >= u, x[u:u + 1, :], 0.0)
    return out


class _PageStream:
    def __init__(self, pt_ref, caches, bufs, sems, *, layer, group, n_pages, reverse):
        self.pt_ref, self.caches, self.bufs, self.sems = pt_ref, caches, bufs, sems
        self.layer, self.group, self.n_pages, self.reverse = layer, group, n_pages, reverse
        self.n_steps = n_pages // group

    def _copies(self, page, slot, g):
        return [pltpu.make_async_copy(c.at[self.layer, page], buf.at[slot, g], sem.at[slot, g])
                for c, buf, sem in zip(self.caches, self.bufs, self.sems)]

    def start(self, b, step, slot):
        for g in range(self.group):
            p = step * self.group + g
            if self.reverse:
                p = self.n_pages - 1 - p
            for cp in self._copies(self.pt_ref[b, p], slot, g):
                cp.start()

    def wait(self, slot):
        for g in range(self.group):
            for cp in self._copies(0, slot, g):
                cp.wait()

    def run(self, compute):
        b = pl.program_id(0)

        @pl.when(b == 0)
        def _():
            self.start(0, 0, 0)

        def body(s, carry):
            slot = lax.rem(s, 2)

            @pl.when(s + 1 < self.n_steps)
            def _():
                self.start(b, s + 1, 1 - slot)

            @pl.when((s + 1 == self.n_steps) & (b + 1 < pl.num_programs(0)))
            def _():
                self.start(b + 1, 0, 0)

            self.wait(slot)
            compute(s, slot)
            return carry

        lax.fori_loop(0, self.n_steps, body, 0)


def _attn_sample_kernel(pt_ref, *refs, mode, layer, group, n_pages, seq, lam_init):
    q_ref, kn_ref, vn_ref, kc_ref, vc_ref = refs[:5]
    rest = refs[5:]
    if mode == 'fox':
        (lfc_ref, combn_ref, o_ref, qx_ref, m_ref, l_ref, acc_ref, sfx_ref,
         kbuf, vbuf, lfbuf, ksem, vsem, lfsem) = rest
        stream = _PageStream(pt_ref, (kc_ref, vc_ref, lfc_ref), (kbuf, vbuf, lfbuf), (ksem, vsem, lfsem),
                             layer=layer, group=group, n_pages=n_pages, reverse=True)
    else:
        if mode == 'diff':
            lam_ref, gd_ref, o_ref, qx_ref, m_ref, l_ref, acc_ref, kbuf, vbuf, ksem, vsem = rest
        else:
            sel_ref, o_ref, qx_ref, m_ref, l_ref, acc_ref, kbuf, vbuf, ksem, vsem = rest
        stream = _PageStream(pt_ref, (kc_ref, vc_ref), (kbuf, vbuf), (ksem, vsem),
                             layer=layer, group=group, n_pages=n_pages, reverse=False)
    n_maps = 2 * N_HEADS if mode == 'diff' else N_HEADS
    w = 256 // n_maps
    rows = n_maps * seq

    q = q_ref[...]
    for g in range(n_maps):
        qx_ref[g * seq:(g + 1) * seq, :] = jnp.where(_lane_mask(256, g * w, (g + 1) * w), q, 0.0)
    m_ref[...] = jnp.full_like(m_ref, NEG)
    l_ref[...] = jnp.zeros_like(l_ref)
    acc_ref[...] = jnp.zeros_like(acc_ref)
    if mode == 'fox':
        sfx_ref[...] = jnp.zeros_like(sfx_ref)

    def new_prefix():
        cn = _prefix_rows(combn_ref[...])
        return cn, jnp.concatenate([cn[:, h:h + 1] for h in range(N_HEADS)], axis=0)

    def update(s_parts, mask, v_parts, nt):
        s = s_parts[0] if len(s_parts) == 1 else jnp.concatenate(s_parts, axis=1)
        pr, m2, alpha, l2 = _online(s, mask, m_ref[...], l_ref[...])
        m_ref[...] = m2
        l_ref[...] = l2
        pv = None
        for g, v in enumerate(v_parts):
            pg = pr[:, g * LANES:(g + 1) * LANES].astype(BF16)
            t = _dot_nt(pg, v) if nt else _dot(pg, v)
            pv = t if pv is None else pv + t
        acc_ref[...] = acc_ref[...] * alpha + pv

    def past_step(s_idx, slot):
        qx = qx_ref[...].astype(BF16)
        s_parts, v_parts = [], []
        if mode == 'fox':
            r = lax.broadcasted_iota(I32, (LANES, LANES), 0)
            c = lax.broadcasted_iota(I32, (LANES, LANES), 1)
            later = jnp.where(r > c, 1.0, 0.0).astype(BF16)
            _, cq = new_prefix()
            lf_rows = jnp.concatenate(
                [jnp.broadcast_to(lfbuf[slot, g, h:h + 1, :], (seq, LANES))
                 for g in range(group) for h in range(N_HEADS)], axis=0)
            a1, a2, a3 = _split3(lf_rows)
            within = _dot(a1, later) + _dot(a2, later) + _dot(a3, later)
            totals = jnp.sum(lf_rows, axis=1, keepdims=True)
            sfx = sfx_ref[...]
        for g in range(group):
            s = _dot(qx, kbuf[slot, g].astype(BF16))
            if mode == 'fox':
                s = s + (cq + (within[g * rows:(g + 1) * rows, :] + sfx)) * LOG2E
                sfx = sfx + totals[g * rows:(g + 1) * rows, :]
            if mode == 'dsa':
                s = s + _tile_rows(sel_ref[s_idx * group + g], n_maps)
            s_parts.append(s)
            v_parts.append(vbuf[slot, g].astype(BF16))
        if mode == 'fox':
            sfx_ref[...] = sfx
        update(s_parts, None, v_parts, True)

    stream.run(past_step)

    qx = qx_ref[...].astype(BF16)
    kpad = _pad_rows(kn_ref[...], LANES).astype(BF16)
    vpad = _pad_rows(vn_ref[...], LANES).astype(BF16)
    s = _dot_nt(qx, kpad)
    qi = lax.broadcasted_iota(I32, (rows, LANES), 0) % seq
    kj = lax.broadcasted_iota(I32, (rows, LANES), 1)
    mask = kj <= qi
    if mode == 'fox':
        cn, cq = new_prefix()
        cnT = _pad_rows(cn, LANES).T
        ck = jnp.concatenate([jnp.broadcast_to(cnT[h:h + 1, :], (seq, LANES)) for h in range(N_HEADS)], axis=0)
        s = s + (cq - ck) * LOG2E
    if mode == 'dsa':
        s = s + _tile_rows(sel_ref[n_pages], n_maps)
    update([s], mask, [vpad], False)
    res = acc_ref[...] / l_ref[...]
    out = jnp.zeros((seq, 256), F32)
    if mode == 'diff':
        lam = _lam_value(lam_ref, lam_init)
        for h in range(N_HEADS):
            o = res[2 * h * seq:(2 * h + 1) * seq, :] - lam * res[(2 * h + 1) * seq:(2 * h + 2) * seq, :]
            hm = _lane_mask(256, h * DH, (h + 1) * DH)
            ms = jnp.sum(jnp.where(hm, o * o, 0.0), axis=1, keepdims=True) * (1.0 / DH)
            o = o * lax.rsqrt(ms + EPS) * gd_ref[...] * (1.0 - lam_init)
            out = out + jnp.where(hm, o, 0.0)
    else:
        for h in range(N_HEADS):
            out = out + jnp.where(_lane_mask(256, h * DH, (h + 1) * DH), res[h * seq:(h + 1) * seq, :], 0.0)
    o_ref[...] = out.astype(o_ref.dtype)


def _page_scratch(n_caches_rows, group):
    bufs = [pltpu.VMEM((2, group, rows, PAGE_SIZE), F32) for rows in n_caches_rows]
    sems = [pltpu.SemaphoreType.DMA((2, group)) for _ in n_caches_rows]
    return bufs + sems


def _attn_sample(mode, page_table, q, kn, vn, cache_kT, cache_vT, extra, *, layer, batch, seq, group, lam_init=0.0):
    n_pages = page_table.shape[1]
    assert (n_pages // group) % 2 == 0 and n_pages % group == 0
    n_maps = 2 * N_HEADS if mode == 'diff' else N_HEADS
    rows = n_maps * seq
    per_b = lambda width: pl.BlockSpec((None, seq, width), lambda b, pt: (b, 0, 0))
    hbm = pl.BlockSpec(memory_space=pl.ANY)
    in_specs = [per_b(256), per_b(256), per_b(256), hbm, hbm]
    args = [q, kn, vn, cache_kT, cache_vT]
    scratch = [pltpu.VMEM((rows, 256), F32), pltpu.VMEM((rows, 1), F32), pltpu.VMEM((rows, 1), F32),
               pltpu.VMEM((rows, 256), F32)]
    if mode == 'fox':
        cache_lfT, comb_new = extra
        in_specs += [hbm, per_b(LANES)]
        args += [cache_lfT, comb_new]
        scratch += [pltpu.VMEM((rows, 1), F32)] + _page_scratch((256, 256, N_HEADS), group)
    elif mode == 'diff':
        lam4, gd = extra
        in_specs += [pl.BlockSpec((4, DC), lambda b, pt: (0, 0)), pl.BlockSpec((1, 256), lambda b, pt: (0, 0))]
        args += [lam4, gd]
        scratch += _page_scratch((256, 256), group)
    else:
        (sel,) = extra
        in_specs += [pl.BlockSpec((None, n_pages + 1, seq, LANES), lambda b, pt: (b, 0, 0, 0))]
        args += [sel]
        scratch += _page_scratch((256, 256), group)
    return pl.pallas_call(
        functools.partial(_attn_sample_kernel, mode=mode, layer=layer, group=group, n_pages=n_pages, seq=seq,
                          lam_init=lam_init),
        grid_spec=pltpu.PrefetchScalarGridSpec(
            num_scalar_prefetch=1, grid=(batch,), in_specs=in_specs,
            out_specs=pl.BlockSpec((None, seq, 256), lambda b, pt: (b, 0, 0)), scratch_shapes=scratch),
        out_shape=jax.ShapeDtypeStruct((batch, seq, 256), F32),
        compiler_params=_cparams(("arbitrary",)), name=mode + "_sample",
    )(page_table, *args)


def _dsa_select_sample_kernel(pt_ref, qi_ref, kin_ref, combn_ref, ikc_ref, sk_ref, qix_ref, ikbuf, iksem, *,
                              layer, group, n_pages, seq):
    stream = _PageStream(pt_ref, (ikc_ref,), (ikbuf,), (iksem,), layer=layer, group=group, n_pages=n_pages,
                         reverse=False)
    qi = qi_ref[...]
    for h in range(N_HEADS):
        qix_ref[h * seq:(h + 1) * seq, :] = _masked_pair(qi, h, DI)[0]

    def scores(logits):
        comb = combn_ref[...]
        sc = jnp.zeros((seq, LANES), F32)
        for h in range(N_HEADS):
            wcol = comb[:, N_HEADS + h:N_HEADS + h + 1] * (DI ** -0.5 * N_HEADS ** -0.5)
            sc = sc + jnp.maximum(logits[h * seq:(h + 1) * seq, :], 0.0) * wcol
        return sc

    def past_step(s_idx, slot):
        qix = qix_ref[...].astype(BF16)
        for g in range(group):
            ikT = ikbuf[slot, g].astype(BF16)
            sk_ref[s_idx * group + g] = _order_key(scores(_dot(qix, jnp.concatenate([ikT, ikT], axis=0))))

    stream.run(past_step)

    qix = qix_ref[...].astype(BF16)
    kin = kin_ref[...]
    kin2 = _pad_rows(kin + pltpu.roll(kin, DI, axis=1), LANES).astype(BF16)
    sc = scores(_dot_nt(qix, kin2))
    qrow = lax.broadcasted_iota(I32, (seq, LANES), 0)
    kcol = lax.broadcasted_iota(I32, (seq, LANES), 1)
    sk_ref[n_pages] = _order_key(jnp.where(kcol <= qrow, sc, -jnp.inf))


def _dsa_pick_sample_kernel(sk_ref, bias_ref, *, k_sel):
    nb, n_blk, seq, _ = sk_ref.shape

    def count_ge(cand):
        ind = jnp.where(sk_ref[...] >= cand, 1.0, 0.0)
        return jnp.sum(jnp.sum(ind, axis=1, keepdims=True), axis=3, keepdims=True)

    thr = _kth_largest_key(count_ge, float(k_sel), (nb, 1, seq, 1))
    keys = sk_ref[...]
    n_gt = jnp.sum(jnp.sum(jnp.where(keys > thr, 1.0, 0.0), axis=1, keepdims=True), axis=3, keepdims=True)
    need = (float(k_sel) - n_gt)[:, 0]
    eq = keys == thr
    eq2d = jnp.where(eq, 1.0, 0.0).reshape(nb * n_blk * seq, LANES).astype(BF16)
    within = _dot(eq2d, _upper_tri(LANES)).reshape(nb, n_blk, seq, LANES)
    seen = jnp.zeros((nb, seq, 1), F32)
    for j in range(n_blk):
        rank = seen + within[:, j]
        sel = (keys[:, j] > thr[:, 0]) | (eq[:, j] & (rank <= need))
        bias_ref[:, j] = jnp.where(sel, 0.0, NEG)
        seen = rank[:, :, LANES - 1:LANES]


def _dsa_select_sample(page_table, qi, ki_new, comb_new, cache_ikT, *, layer, batch, seq, group):
    n_pages = page_table.shape[1]
    assert (n_pages // group) % 2 == 0 and n_pages % group == 0
    per_b = lambda width: pl.BlockSpec((None, seq, width), lambda b, pt: (b, 0, 0))
    in_specs = [per_b(256), per_b(LANES), per_b(LANES), pl.BlockSpec(memory_space=pl.ANY)]
    k_sel = min(TOPK_MAX, (n_pages * PAGE_SIZE + seq) // 4)
    keys = pl.pallas_call(
        functools.partial(_dsa_select_sample_kernel, layer=layer, group=group, n_pages=n_pages, seq=seq),
        grid_spec=pltpu.PrefetchScalarGridSpec(
            num_scalar_prefetch=1, grid=(batch,), in_specs=in_specs,
            out_specs=pl.BlockSpec((None, n_pages + 1, seq, LANES), lambda b, pt: (b, 0, 0, 0)),
            scratch_shapes=[pltpu.VMEM((N_HEADS * seq, LANES), F32)] + _page_scratch((DI,), group)),
        out_shape=jax.ShapeDtypeStruct((batch, n_pages + 1, seq, LANES), I32),
        compiler_params=_cparams(("arbitrary",)), name="dsa_score_sample",
    )(page_table, qi, ki_new, comb_new, cache_ikT)
    shape = (batch, n_pages + 1, seq, LANES)
    whole = pl.BlockSpec(shape, lambda i: (0, 0, 0, 0))
    return pl.pallas_call(
        functools.partial(_dsa_pick_sample_kernel, k_sel=k_sel), grid=(1,), in_specs=[whole], out_specs=whole,
        out_shape=jax.ShapeDtypeStruct(shape, F32), compiler_params=_cparams(("arbitrary",)), name="dsa_pick_sample",
    )(keys)


def _mix_kernel(x_ref, oa_ref, ob_ref, oc_ref, od_ref, gpre_ref, wg_ref, bg_ref, wpa_ref, wpb_ref, wpc_ref,
                wpd_ref, wo_ref, gpost_ref, y_ref):
    x = x_ref[...]
    h = _rms(x, gpre_ref[...]).astype(BF16)
    mixed = None
    for n, (o_ref, wp_ref) in enumerate(((oa_ref, wpa_ref), (ob_ref, wpb_ref), (oc_ref, wpc_ref), (od_ref, wpd_ref))):
        gate = jax.nn.sigmoid(_dot(h, wg_ref[:, n * D_MODEL:(n + 1) * D_MODEL]) + bg_ref[:, n * D_MODEL:(n + 1) * D_MODEL])
        t = gate * _dot(o_ref[...].astype(BF16), wp_ref[...])
        mixed = t if mixed is None else mixed + t
    y = _dot(mixed.astype(BF16), wo_ref[...])
    y_ref[...] = x + _rms(y, gpost_ref[...])


def _mix(x2d, oa, ob, oc, od, gpre, wg, bg, wpa, wpb, wpc, wpd, wo, gpost, *, tm):
    n = x2d.shape[0]
    tile = lambda width: pl.BlockSpec((tm, width), lambda i: (i, 0))
    const = lambda a: pl.BlockSpec(a.shape, lambda i: (0, 0))
    consts = (gpre, wg, bg, wpa, wpb, wpc, wpd, wo, gpost)
    return pl.pallas_call(
        _mix_kernel, grid=(n // tm,),
        in_specs=[tile(D_MODEL), tile(512), tile(256), tile(256), tile(256)] + [const(a) for a in consts],
        out_specs=tile(D_MODEL), out_shape=jax.ShapeDtypeStruct((n, D_MODEL), F32),
        compiler_params=_cparams(("parallel",)), name="mix",
    )(x2d, oa, ob, oc, od, *consts)


def _ffn_kernel(x_ref, gpre_ref, wgu_ref, wd_ref, gpost_ref, y_ref):
    x = x_ref[...]
    h = _rms(x, gpre_ref[...]).astype(BF16)
    g = _dot(h, wgu_ref[:, :D_FF])
    u = _dot(h, wgu_ref[:, D_FF:])
    a = (g * jax.nn.sigmoid(g) * u).astype(BF16)
    y_ref[...] = x + _rms(_dot(a, wd_ref[...]), gpost_ref[...])


def _ffn(x2d, gpre, wgu, wd, gpost, *, tm):
    n = x2d.shape[0]
    tile = pl.BlockSpec((tm, D_MODEL), lambda i: (i, 0))
    const = lambda a: pl.BlockSpec(a.shape, lambda i: (0, 0))
    consts = (gpre, wgu, wd, gpost)
    return pl.pallas_call(
        _ffn_kernel, grid=(n // tm,), in_specs=[tile] + [const(a) for a in consts],
        out_specs=tile, out_shape=jax.ShapeDtypeStruct((n, D_MODEL), F32),
        compiler_params=_cparams(("parallel",)), name="ffn",
    )(x2d, *consts)


def _rope_tables(pos):
    def one(headw, rot, theta):
        half = rot // 2
        inv = theta ** (-jnp.arange(half, dtype=F32) / half)
        ang = pos.astype(F32)[:, None] * inv[None, :]
        cos, sin = jnp.cos(ang), jnp.sin(ang)
        rest = headw - rot
        n = pos.shape[0]
        c = jnp.concatenate([cos, cos, jnp.ones((n, rest), F32)], axis=1)
        s = jnp.concatenate([-sin, sin, jnp.zeros((n, rest), F32)], axis=1)
        return jnp.tile(c, (1, 256 // headw)), jnp.tile(s, (1, 256 // headw))
    cr, sr = one(DKA, DKA, RET_THETA)
    c32, s32 = one(DC, DC // ROPE_FRAC, ROPE_THETA)
    c64, s64 = one(DH, DH // ROPE_FRAC, ROPE_THETA)
    return (cr, sr, c32, s32, c64, s64)


def _pack_w_in(w_in):
    depth = w_in.shape[0]
    seg = lambda name: w_in[:, :, _SRC[name][0]:_SRC[name][0] + _SRC[name][1]]
    zeros = lambda width: jnp.zeros((depth, D_MODEL, width), w_in.dtype)
    parts = [seg(n) for n in _PACK_ORDER]
    parts += [seg('ki'), zeros(LANES - DI), seg('fb'), seg('wi'), zeros(LANES - 2 * N_HEADS)]
    return jnp.concatenate(parts, axis=2).astype(BF16)


def _heads_T(a, batch, seq):
    return a.reshape(batch, N_HEADS, a.shape[1] // N_HEADS, seq).transpose(0, 3, 1, 2)


def kernel(x_prompt, x_sample, state_ret, cache_fox_k, cache_fox_v, cache_fox_lf, cache_diff_k, cache_diff_v, cache_dsa_k, cache_dsa_v, cache_dsa_ik, page_table, g_pre_mix, g_post_mix, g_pre_ffn, g_post_ffn, w_in, b_forget, lam_q1, lam_k1, lam_q2, lam_k2, g_diff, w_pa, w_pb, w_pc, w_pd, w_gate, b_gate, w_out, w_gu, w_down):
    bp, tp, _ = x_prompt.shape
    bs, ts, _ = x_sample.shape
    depth = w_in.shape[0]
    n_pool = cache_fox_k.shape[1]
    past = page_table.shape[1] * PAGE_SIZE
    tm_p = min(256, tp)
    tb = min(256, tp)
    tk = min(256, tp)
    group = min(16, page_table.shape[1] // 2)

    w_pack = _pack_w_in(w_in)
    bf_pad = jnp.pad(b_forget, ((0, 0), (0, LANES - N_HEADS)))[:, None, :]
    tabs_p = _rope_tables(jnp.arange(tp, dtype=I32))
    tabs_s = _rope_tables(jnp.tile(past + jnp.arange(ts, dtype=I32), bs))
    row = lambda a, l: a[l][None, :]
    bf = lambda a: a.astype(BF16)
    wg16, wpa16, wpb16, wpc16, wpd16, wo16, wgu16, wd16 = map(bf, (w_gate, w_pa, w_pb, w_pc, w_pd, w_out, w_gu, w_down))
    lam4 = jnp.stack([lam_q1, lam_k1, lam_q2, lam_k2], axis=1)
    pageT = lambda c: jnp.moveaxis(c.reshape(depth, n_pool, PAGE_SIZE, -1), 2, 3)
    fkT, fvT, ckT, cvT, dkT, dvT, ikT, lfT = map(pageT, (cache_fox_k, cache_fox_v, cache_diff_k, cache_diff_v,
                                                         cache_dsa_k, cache_dsa_v, cache_dsa_ik, cache_fox_lf))

    xp = x_prompt.reshape(bp * tp, D_MODEL)
    xs = x_sample.reshape(bs * ts, D_MODEL)
    new = {n: [] for n in ('ret_p', 'ret_s', 'fk_p', 'fv_p', 'flf_p', 'fk_s', 'fv_s', 'flf_s', 'ck_p', 'cv_p',
                           'ck_s', 'cv_s', 'dk_p', 'dv_p', 'dik_p', 'dk_s', 'dv_s', 'dik_s')}
    for l in range(depth):
        lam_init = 0.8 - 0.6 * math.exp(-0.3 * l)
        finish = lambda x, oa, ob, oc, od, tm: _ffn(
            _mix(x, oa, ob, oc, od, row(g_pre_mix, l), wg16[l], row(b_gate, l), wpa16[l], wpb16[l], wpc16[l],
                 wpd16[l], wo16[l], row(g_post_mix, l), tm=tm),
            row(g_pre_ffn, l), wgu16[l], wd16[l], row(g_post_ffn, l), tm=tm)
        pr = _proj(xp, row(g_pre_mix, l), w_pack[l], bf_pad[l], tabs_p, mode='prompt', batch=bp, seq=tp, tm=tm_p)
        oa, r_p = _ret_prompt(pr['qa'], pr['ka'], pr['va'], pr['ga'], batch=bp, seq=tp)
        tiles = dict(batch=bp, seq=tp, tb=tb, tk=tk)
        ob = _attn_prompt('fox', pr['qb'], pr['kb'], pr['vb'], (pr['comb'], pr['auxT']), **tiles)
        oc = _attn_prompt('diff', pr['qc'], pr['kc'], pr['vc'], (lam4[l], row(g_diff, l)), lam_init=lam_init, **tiles)
        od = _attn_prompt('dsa', pr['qd'], pr['kd'], pr['vd'], (pr['qi'], pr['ki'], pr['comb']), **tiles)
        xp = finish(xp, oa, ob, oc, od, tm_p)
        new['ret_p'].append(r_p.reshape(bp, N_HEADS, DKA, DVA))
        for name, key in (('fk_p', 'kb'), ('fv_p', 'vb'), ('ck_p', 'kc'), ('cv_p', 'vc'), ('dk_p', 'kd'), ('dv_p', 'vd')):
            new[name].append(_heads_T(pr[key], bp, tp))
        new['flf_p'].append(jnp.swapaxes(pr['auxT'][:, :N_HEADS, :], 1, 2))
        new['dik_p'].append(jnp.swapaxes(pr['ki'], 1, 2))
        sm = _proj(xs, row(g_pre_mix, l), w_pack[l], bf_pad[l], tabs_s, mode='sample', batch=bs, seq=ts, tm=bs * ts)
        oa, r_s = _ret_sample(sm['qa'], sm['ka'], sm['va'], sm['ga'], state_ret[l].reshape(bs, 256, DVA), batch=bs, seq=ts)
        per_b = lambda a: a.reshape(bs, ts, a.shape[-1])
        common = dict(layer=l, batch=bs, seq=ts, group=group)
        ob = _attn_sample('fox', page_table, per_b(sm['qb']), per_b(sm['kb']), per_b(sm['vb']), fkT, fvT,
                          (lfT, per_b(sm['comb'])), **common)
        oc = _attn_sample('diff', page_table, per_b(sm['qc']), per_b(sm['kc']), per_b(sm['vc']), ckT, cvT,
                          (lam4[l], row(g_diff, l)), lam_init=lam_init, **common)
        sel = _dsa_select_sample(page_table, per_b(sm['qi']), per_b(sm['ki']), per_b(sm['comb']), ikT, **common)
        od = _attn_sample('dsa', page_table, per_b(sm['qd']), per_b(sm['kd']), per_b(sm['vd']), dkT, dvT, (sel,), **common)
        flat = lambda a: a.reshape(bs * ts, 256)
        xs = finish(xs, oa, flat(ob), flat(oc), flat(od), bs * ts)
        new['ret_s'].append(r_s.reshape(bs, N_HEADS, DKA, DVA))
        for name, key in (('fk_s', 'kb'), ('fv_s', 'vb'), ('ck_s', 'kc'), ('cv_s', 'vc'), ('dk_s', 'kd'), ('dv_s', 'vd')):
            new[name].append(sm[key].reshape(bs, ts, N_HEADS, DH))
        new['flf_s'].append(sm['comb'][:, :N_HEADS].reshape(bs, ts, N_HEADS))
        new['dik_s'].append(sm['ki'][:, :DI].reshape(bs, ts, DI))
    st = lambda n: jnp.stack(new[n], axis=0)
    return (xp.reshape(bp, tp, D_MODEL), xs.reshape(bs, ts, D_MODEL), st('ret_p'), st('ret_s'),
            st('fk_p'), st('fv_p'), st('flf_p'), st('fk_s'), st('fv_s'), st('flf_s'),
            st('ck_p'), st('cv_p'), st('ck_s'), st('cv_s'),
            st('dk_p'), st('dv_p'), st('dik_p'), st('dk_s'), st('dv_s'), st('dik_s'))
```

```python
import functools
import math

import jax
import jax.numpy as jnp
from jax import lax
from jax.experimental import pallas as pl
from jax.experimental.pallas import tpu as pltpu

F32 = jnp.float32
BF16 = jnp.bfloat16
I32 = jnp.int32

D_MODEL = 1024
HEAD_DIM = 64
N_HEADS = 4
DKA = HEAD_DIM
DVA = 2 * HEAD_DIM
RET_CHUNK = 128
RET_THETA = 10000.0
DH = HEAD_DIM
DC = HEAD_DIM // 2
DI = HEAD_DIM
TOPK_MAX = 256
ROPE_THETA = 500000.0
ROPE_FRAC = 4
D_FF = 2816
EPS = 1e-6
PAGE_SIZE = 128
LANES = 128
NEG = -1e30
INT_MIN = -2 ** 31
LOG2E = math.log2(math.e)
VMEM_LIMIT = 56 * 1024 * 1024

LOG_GAMMA = tuple(math.log1p(-(2.0 ** (-5.0 - h))) for h in range(N_HEADS))

_SRC = dict(qa=(0, 256), ka=(256, 256), va=(512, 512), ga=(1024, 512),
            qb=(1536, 256), kb=(1792, 256), vb=(2048, 256), fb=(2304, 4),
            qc=(2308, 256), kc=(2564, 256), vc=(2820, 256),
            qd=(3076, 256), kd=(3332, 256), vd=(3588, 256),
            qi=(3844, 256), ki=(4100, 64), wi=(4164, 4))
_PACK_ORDER = ('qa', 'ka', 'va', 'ga', 'qb', 'kb', 'vb', 'qc', 'kc', 'vc', 'qd', 'kd', 'vd', 'qi')
_GRP = {}
_off = 0
for _n in _PACK_ORDER:
    _GRP[_n] = (_off, _SRC[_n][1])
    _off += _SRC[_n][1]
_GRP['ki'] = (_off, LANES)
_off += LANES
_GRP['aux'] = (_off, LANES)
_off += LANES
N_PACK = _off


def _cparams(sem):
    return pltpu.CompilerParams(dimension_semantics=sem, vmem_limit_bytes=VMEM_LIMIT)


def _rms(x, g=None):
    y = x * lax.rsqrt(jnp.mean(x * x, axis=-1, keepdims=True) + EPS)
    return y if g is None else y * g


def _dot(a, b):
    return jnp.dot(a, b, preferred_element_type=F32)


def _dot_nt(a, b):
    return lax.dot_general(a, b, (((1,), (1,)), ((), ())), preferred_element_type=F32)


def _split3(x):
    x1 = x.astype(BF16)
    r1 = x - x1.astype(F32)
    x2 = r1.astype(BF16)
    x3 = (r1 - x2.astype(F32)).astype(BF16)
    return x1, x2, x3


def _lane_mask(width, lo, hi):
    lane = lax.broadcasted_iota(I32, (1, width), 1)
    return (lane >= lo) & (lane < hi)


def _masked_pair(q, g, w):
    p = (g * w) // LANES
    lo = g * w - p * LANES
    qp = q[:, p * LANES:(p + 1) * LANES]
    return jnp.where(_lane_mask(LANES, lo, lo + w), qp, jnp.zeros_like(qp)), p


def _order_key(score):
    bits = lax.bitcast_convert_type(score + 0.0, I32)
    return jnp.where(bits < 0, bits ^ jnp.int32(0x7FFFFFFF), bits)


def _kth_largest_key(count_ge, k, shape, bits=32):
    nonneg = count_ge(jnp.zeros(shape, I32)) >= k
    lo0 = jnp.where(nonneg, jnp.int32(0), jnp.int32(-2 ** (bits - 1)))

    def body(b, lo):
        cand = lo + jnp.left_shift(jnp.int32(1), bits - 2 - b)
        return jnp.where(count_ge(cand) >= k, cand, lo)

    return lax.fori_loop(0, bits - 1, body, lo0)


_PROMPT_OUTS = (('qa', 'tok', BF16), ('ka', 'T', BF16), ('va', 'tok', BF16), ('ga', 'tok', F32),
                ('qb', 'tok', BF16), ('kb', 'T', F32), ('vb', 'T', F32),
                ('qc', 'tok', BF16), ('kc', 'T', F32), ('vc', 'T', F32),
                ('qd', 'tok', BF16), ('kd', 'T', F32), ('vd', 'T', F32),
                ('qi', 'tok', BF16), ('ki', 'T', F32), ('comb', 'tok', F32), ('auxT', 'T', F32))
_SAMPLE_OUTS = (('qa', 'tok', F32), ('ka', 'T', F32), ('va', 'tok', F32), ('ga', 'tok', F32),
                ('qb', 'tok', F32), ('kb', 'tok', F32), ('vb', 'tok', F32),
                ('qc', 'tok', F32), ('kc', 'tok', F32), ('vc', 'tok', F32),
                ('qd', 'tok', F32), ('kd', 'tok', F32), ('vd', 'tok', F32),
                ('qi', 'tok', F32), ('ki', 'tok', F32), ('comb', 'tok', F32))
_ROPE_KIND = dict(qa='ret', ka='ret', qc='r32', kc='r32', qd='r64', kd='r64', qi='r64', ki='r64')
_SCALE = dict(ka=DKA ** -0.5, qb=DH ** -0.5 * LOG2E, qc=DC ** -0.5 * LOG2E, qd=DH ** -0.5 * LOG2E)
_T_ROWS = dict(ki=DI, auxT=16)


def _proj_kernel(*refs, outs, with_cumsum, tiles_per_seq):
    (x_ref, g_ref, w_ref, bf_ref, cr_ref, sr_ref, c32_ref, s32_ref, c64_ref, s64_ref) = refs[:10]
    out_refs = dict(zip([o[0] for o in outs], refs[10:10 + len(outs)]))
    tm = x_ref.shape[0]
    h = _rms(x_ref[...], g_ref[...]).astype(BF16)
    tables = dict(ret=(cr_ref, sr_ref, DKA, DKA // 2),
                  r32=(c32_ref, s32_ref, DC, DC // ROPE_FRAC // 2),
                  r64=(c64_ref, s64_ref, DH, DH // ROPE_FRAC // 2))

    def group(name):
        start, width = _GRP[name]
        p = _dot(h, w_ref[:, start:start + width])
        kind = _ROPE_KIND.get(name)
        if kind is not None:
            c_ref, s_ref, headw, half = tables[kind]
            lane = lax.broadcasted_iota(I32, (1, width), 1)
            first = (lane & (headw - 1)) < half
            rx = jnp.where(first, pltpu.roll(p, width - half, axis=1), pltpu.roll(p, half, axis=1))
            p = p * c_ref[:, :width] + rx * s_ref[:, :width]
        if name in _SCALE:
            p = p * _SCALE[name]
        return p

    comb = None
    for name, layout, dtype in outs:
        ref = out_refs[name]
        if name in ('comb', 'auxT'):
            if comb is None:
                a = group('aux')
                z = a + bf_ref[...]
                lf = jnp.minimum(z, 0.0) - jnp.log1p(jnp.exp(-jnp.abs(z)))
                lane = lax.broadcasted_iota(I32, (1, LANES), 1)
                comb = jnp.where(lane < N_HEADS, lf, a)
                if with_cumsum:
                    carry_ref = refs[-1]

                    @pl.when(pl.program_id(0) % tiles_per_seq == 0)
                    def _():
                        carry_ref[...] = jnp.zeros_like(carry_ref)

                    r = lax.broadcasted_iota(I32, (tm, tm), 0)
                    c = lax.broadcasted_iota(I32, (tm, tm), 1)
                    tri = jnp.where(c <= r, 1.0, 0.0).astype(BF16)
                    l1, l2, l3 = _split3(lf)
                    cs = _dot(tri, l1) + _dot(tri, l2) + _dot(tri, l3) + carry_ref[...]
                    carry_ref[...] = cs[tm - 1:tm, :]
                    comb = jnp.where(lane < 2 * N_HEADS, comb,
                                     jnp.where(lane < 3 * N_HEADS, pltpu.roll(cs, 2 * N_HEADS, axis=1), 0.0))
            val = comb
        else:
            val = group(name)
        if layout == 'tok':
            ref[...] = val.astype(dtype)
        else:
            vt = val.T
            rows = _T_ROWS.get(name, vt.shape[0])
            ref[...] = vt[:rows, :].astype(dtype)


def _proj(x2d, gain, w_pack, bf_pad, tabs, *, mode, batch, seq, tm):
    n = x2d.shape[0]
    outs = _PROMPT_OUTS if mode == 'prompt' else _SAMPLE_OUTS
    tps = seq // tm if mode == 'prompt' else 1
    n_tiles = n // tm
    tab_rows = tabs[0].shape[0]
    tab_tiles = tab_rows // tm
    in_specs = [pl.BlockSpec((tm, D_MODEL), lambda i: (i, 0)),
                pl.BlockSpec((1, D_MODEL), lambda i: (0, 0)),
                pl.BlockSpec((D_MODEL, N_PACK), lambda i: (0, 0)),
                pl.BlockSpec((1, LANES), lambda i: (0, 0))]
    in_specs += [pl.BlockSpec((tm, 256), lambda i: (i % tab_tiles, 0)) for _ in range(6)]
    out_specs, out_shapes = [], []
    for name, layout, dtype in outs:
        width = LANES if name in ('comb', 'auxT', 'ki') else _GRP[name][1]
        if layout == 'tok':
            out_specs.append(pl.BlockSpec((tm, width), lambda i: (i, 0)))
            out_shapes.append(jax.ShapeDtypeStruct((n, width), dtype))
        else:
            rows = _T_ROWS.get(name, width)
            if mode == 'prompt':
                out_specs.append(pl.BlockSpec((None, rows, tm), lambda i: (i // tps, 0, i % tps)))
                out_shapes.append(jax.ShapeDtypeStruct((batch, rows, seq), dtype))
            else:
                out_specs.append(pl.BlockSpec((rows, tm), lambda i: (0, i)))
                out_shapes.append(jax.ShapeDtypeStruct((rows, n), dtype))
    res = pl.pallas_call(
        functools.partial(_proj_kernel, outs=outs, with_cumsum=(mode == 'prompt'), tiles_per_seq=tps),
        grid=(n_tiles,), in_specs=in_specs, out_specs=out_specs, out_shape=out_shapes,
        scratch_shapes=[pltpu.VMEM((1, LANES), F32)],
        compiler_params=_cparams(("arbitrary",)), name="proj_" + mode,
    )(x2d, gain, w_pack, bf_pad, *tabs)
    return dict(zip([o[0] for o in outs], res))


def _ret_finish(o, ga_h):
    return _rms(o) * (ga_h * jax.nn.sigmoid(ga_h))


def _ret_prompt_kernel(q_ref, kT_ref, v_ref, ga_ref, o_ref, st_ref, r_ref):
    c = q_ref.shape[0]
    j = pl.program_id(1)

    @pl.when(j == 0)
    def _():
        r_ref[...] = jnp.zeros_like(r_ref)

    q = q_ref[...]
    row = lax.broadcasted_iota(I32, (c, c), 0)
    col = lax.broadcasted_iota(I32, (c, c), 1)
    rel = (row - col).astype(F32)
    irow = lax.broadcasted_iota(I32, (c, 1), 0).astype(F32)
    icol = lax.broadcasted_iota(I32, (1, c), 1).astype(F32)
    for h in range(N_HEADS):
        lg = LOG_GAMMA[h]
        dmask = jnp.where(rel >= 0, jnp.exp(jnp.maximum(rel, 0.0) * lg), 0.0)
        xi = jnp.exp((irow + 1.0) * lg)
        zeta = jnp.exp((c - 1.0 - icol) * lg)
        qp, p = _masked_pair(q, h, DKA)
        s = _dot(qp, kT_ref[p * LANES:(p + 1) * LANES, :]) * dmask
        v_h = v_ref[:, h * DVA:(h + 1) * DVA]
        r_pair = r_ref[p * LANES:(p + 1) * LANES, :].astype(BF16)
        o = _dot(s.astype(BF16), v_h) + _dot((qp.astype(F32) * xi).astype(BF16), r_pair)
        kz = (kT_ref[h * DKA:(h + 1) * DKA, :].astype(F32) * zeta).astype(BF16)
        r_ref[h * DKA:(h + 1) * DKA, :] = math.exp(c * lg) * r_ref[h * DKA:(h + 1) * DKA, :] + _dot(kz, v_h)
        o_ref[:, h * DVA:(h + 1) * DVA] = _ret_finish(o, ga_ref[:, h * DVA:(h + 1) * DVA]).astype(o_ref.dtype)

    @pl.when(j == pl.num_programs(1) - 1)
    def _():
        st_ref[...] = r_ref[...]


def _ret_prompt(qa, kaT, va, ga, *, batch, seq):
    c = RET_CHUNK
    nc = seq // c
    return pl.pallas_call(
        _ret_prompt_kernel, grid=(batch, nc),
        in_specs=[pl.BlockSpec((c, 256), lambda b, j: (b * nc + j, 0)),
                  pl.BlockSpec((None, 256, c), lambda b, j: (b, 0, j)),
                  pl.BlockSpec((c, 512), lambda b, j: (b * nc + j, 0)),
                  pl.BlockSpec((c, 512), lambda b, j: (b * nc + j, 0))],
        out_specs=[pl.BlockSpec((c, 512), lambda b, j: (b * nc + j, 0)),
                   pl.BlockSpec((None, 256, DVA), lambda b, j: (b, 0, 0))],
        out_shape=[jax.ShapeDtypeStruct((batch * seq, 512), BF16),
                   jax.ShapeDtypeStruct((batch, 256, DVA), F32)],
        scratch_shapes=[pltpu.VMEM((256, DVA), F32)],
        compiler_params=_cparams(("parallel", "arbitrary")), name="ret_prompt",
    )(qa, kaT, va, ga)


def _ret_sample_kernel(q_ref, kT_ref, v_ref, ga_ref, st_ref, o_ref, stn_ref, *, batch, seq):
    n = batch * seq
    q = q_ref[...]
    row = lax.broadcasted_iota(I32, (n, n), 0)
    col = lax.broadcasted_iota(I32, (n, n), 1)
    same = (row // seq) == (col // seq)
    rel = ((row % seq) - (col % seq)).astype(F32)
    irow = (lax.broadcasted_iota(I32, (n, 1), 0) % seq).astype(F32)
    icol_i = lax.broadcasted_iota(I32, (1, n), 1)
    icol = (icol_i % seq).astype(F32)
    seq_lane = lax.broadcasted_iota(I32, (1, batch * LANES), 1) // LANES
    tok_row = lax.broadcasted_iota(I32, (n, 1), 0) // seq
    own_state = seq_lane == tok_row
    seq_sub = lax.broadcasted_iota(I32, (batch * DKA, 1), 0) // DKA
    own_tok = seq_sub == (icol_i // seq)
    for h in range(N_HEADS):
        lg = LOG_GAMMA[h]
        dmask = jnp.where(same & (rel >= 0), jnp.exp(jnp.maximum(rel, 0.0) * lg), 0.0)
        xi = jnp.exp((irow + 1.0) * lg)
        zeta = jnp.exp((seq - 1.0 - icol) * lg)
        qp, p = _masked_pair(q, h, DKA)
        kT_pair = kT_ref[p * LANES:(p + 1) * LANES, :].astype(BF16)
        s = _dot(qp.astype(BF16), kT_pair) * dmask
        v_h = v_ref[:, h * DVA:(h + 1) * DVA].astype(BF16)
        qx = (qp * xi).astype(BF16)
        q_big = jnp.where(own_state, jnp.concatenate([qx] * batch, axis=1), jnp.zeros((), BF16))
        r_pair = st_ref[:, p * LANES:(p + 1) * LANES, :].reshape(batch * LANES, DVA).astype(BF16)
        o = _dot(s.astype(BF16), v_h) + _dot(q_big, r_pair)
        o_ref[:, h * DVA:(h + 1) * DVA] = _ret_finish(o, ga_ref[:, h * DVA:(h + 1) * DVA]).astype(o_ref.dtype)
        kz = (kT_ref[h * DKA:(h + 1) * DKA, :] * zeta).astype(BF16)
        k_big = jnp.where(own_tok, jnp.concatenate([kz] * batch, axis=0), jnp.zeros((), BF16))
        upd = _dot(k_big, v_h).reshape(batch, DKA, DVA)
        stn_ref[:, h * DKA:(h + 1) * DKA, :] = math.exp(seq * lg) * st_ref[:, h * DKA:(h + 1) * DKA, :] + upd


def _ret_sample(qa, kaT, va, ga, state, *, batch, seq):
    n = batch * seq
    full = lambda shape: pl.BlockSpec(shape, lambda i: (0,) * len(shape))
    return pl.pallas_call(
        functools.partial(_ret_sample_kernel, batch=batch, seq=seq), grid=(1,),
        in_specs=[full((n, 256)), full((256, n)), full((n, 512)), full((n, 512)), full((batch, 256, DVA))],
        out_specs=[full((n, 512)), full((batch, 256, DVA))],
        out_shape=[jax.ShapeDtypeStruct((n, 512), BF16), jax.ShapeDtypeStruct((batch, 256, DVA), F32)],
        compiler_params=_cparams(("arbitrary",)), name="ret_sample",
    )(qa, kaT, va, ga, state)


def _lam_value(lam_ref, lam_init):
    l4 = lam_ref[...]
    s1 = jnp.sum(l4[0:1, :] * l4[1:2, :], axis=1, keepdims=True)
    s2 = jnp.sum(l4[2:3, :] * l4[3:4, :], axis=1, keepdims=True)
    return jnp.exp(s1) - jnp.exp(s2) + lam_init


def _online(s, mask, m, l):
    if mask is not None:
        s = jnp.where(mask, s, NEG)
    m2 = jnp.maximum(m, jnp.max(s, axis=1, keepdims=True))
    alpha = jnp.exp2(m - m2)
    p = jnp.exp2(s - m2)
    if mask is not None:
        p = jnp.where(mask, p, 0.0)
    return p, m2, alpha, l * alpha + jnp.sum(p, axis=1, keepdims=True)


def _upper_tri(n):
    r = lax.broadcasted_iota(I32, (n, n), 0)
    c = lax.broadcasted_iota(I32, (n, n), 1)
    return jnp.where(r <= c, 1.0, 0.0).astype(BF16)


def _attn_prompt_kernel(*refs, mode, tb, tk, k_sel, lam_init):
    i = pl.program_id(1)
    n_full = (i * tb) // tk
    m_ref, acc_ref, p_ref = refs[-3:]
    refs = refs[:-3]
    if mode == 'fox':
        q_ref, kT_ref, vT_ref, comb_ref, auxT_ref, o_ref, cq_ref = refs
    elif mode == 'diff':
        q_ref, kT_ref, vT_ref, lam_ref, gd_ref, o_ref = refs
    else:
        q_ref, kT_ref, vT_ref, qi_ref, kiT_ref, comb_ref, o_ref, sk_ref, hi_ref, lo_ref = refs
    q = q_ref[...]
    row = lax.broadcasted_iota(I32, (tb, tk), 0)
    col = lax.broadcasted_iota(I32, (tb, tk), 1)
    causal = (n_full * tk + col) <= (i * tb + row)

    def blk(j):
        return pl.ds(pl.multiple_of(j * tk, tk), tk)

    if mode == 'dsa':
        qi = qi_ref[...]
        comb = comb_ref[...]
        qips = [_masked_pair(qi, h, DI)[0] for h in range(N_HEADS)]
        wcols = [comb[:, N_HEADS + h:N_HEADS + h + 1] * (DI ** -0.5 * N_HEADS ** -0.5) for h in range(N_HEADS)]

        def score_block(j, diag):
            kiT = kiT_ref[:, blk(j)].astype(BF16)
            kiT2 = jnp.concatenate([kiT, kiT], axis=0)
            sc = jnp.zeros((tb, tk), F32)
            for h in range(N_HEADS):
                sc = sc + jnp.maximum(_dot(qips[h], kiT2), 0.0) * wcols[h]
            if diag:
                sc = jnp.where(causal, sc, -jnp.inf)
            key = _order_key(sc)
            sk_ref[j] = key
            hi_ref[j] = jnp.right_shift(key, 16).astype(jnp.int16)

        def score_body(j, carry):
            score_block(j, False)
            return carry

        lax.fori_loop(0, n_full, score_body, 0)
        score_block(n_full, True)

        def count16(ref):
            def count_ge(cand):
                cand16 = jnp.broadcast_to(cand, (tb, LANES)).astype(jnp.int16)

                def body(j, acc):
                    for c in range(tk // LANES):
                        hit = ref[j, :, c * LANES:(c + 1) * LANES] >= cand16
                        acc = acc + jnp.where(hit, jnp.int16(1), jnp.int16(0))
                    return acc
                acc = lax.fori_loop(0, n_full + 1, body, jnp.zeros((tb, LANES), jnp.int16))
                return jnp.sum(acc.astype(F32), axis=1, keepdims=True)
            return count_ge

        thr_hi = _kth_largest_key(count16(hi_ref), float(k_sel), (tb, 1), bits=16)

        def low_block(j, carry):
            key = sk_ref[j]
            hi = jnp.right_shift(key, 16)
            lo = jnp.bitwise_and(key, 0xFFFF) - 32768
            lo_ref[j] = jnp.where(hi == thr_hi, lo, jnp.where(hi > thr_hi, 32767, -32768)).astype(jnp.int16)
            return carry

        lax.fori_loop(0, n_full + 1, low_block, 0)
        thr_lo = _kth_largest_key(count16(lo_ref), float(k_sel), (tb, 1), bits=16)
        thr = thr_hi * 65536 + (thr_lo + 32768)

        def gt_body(j, acc):
            ind = jnp.where(sk_ref[j] > thr, 1.0, 0.0)
            for c in range(tk // LANES):
                acc = acc + ind[:, c * LANES:(c + 1) * LANES]
            return acc
        n_gt = jnp.sum(lax.fori_loop(0, n_full + 1, gt_body, jnp.zeros((tb, LANES), F32)), axis=1, keepdims=True)
        need = float(k_sel) - n_gt
        tw = min(tk, 256)
        tri = _upper_tri(tw)

        def sel_block(j, seen, diag):
            x = sk_ref[j]
            eq = x == thr
            eqf = jnp.where(eq, 1.0, 0.0).astype(BF16)
            ranks = []
            for c in range(tk // tw):
                ranks.append(seen + _dot(eqf[:, c * tw:(c + 1) * tw], tri))
                seen = ranks[-1][:, tw - 1:tw]
            rank = ranks[0] if len(ranks) == 1 else jnp.concatenate(ranks, axis=1)
            sel = (x > thr) | (eq & (rank <= need))
            if diag:
                sel = sel & causal
            sk_ref[j] = lax.bitcast_convert_type(jnp.where(sel, 0.0, NEG), I32)
            return seen

        seen = lax.fori_loop(0, n_full, lambda j, s: sel_block(j, s, False), jnp.zeros((tb, 1), F32))
        sel_block(n_full, seen, True)

    n_maps = 2 * N_HEADS if mode == 'diff' else N_HEADS
    w = 256 // n_maps
    qps = [_masked_pair(q, g, w) for g in range(n_maps)]
    if mode == 'fox':
        for g in range(n_maps):
            cq_ref[g] = jnp.broadcast_to(comb_ref[:, 2 * N_HEADS + g:2 * N_HEADS + g + 1] * LOG2E, (tb, LANES))
    m_ref[...] = jnp.full_like(m_ref, NEG)
    acc_ref[...] = jnp.zeros_like(acc_ref)
    ones_row = jnp.where(lax.broadcasted_iota(I32, (LANES - DH, tk), 0) == 0, 1.0, 0.0).astype(BF16)
    n_sub = tk // LANES

    def values(j):
        return [jnp.concatenate([vT_ref[h * DH:(h + 1) * DH, blk(j)].astype(BF16), ones_row], axis=0)
                for h in range(N_HEADS)]

    p_ref[...] = jnp.zeros_like(p_ref)

    def step(j, diag):
        kTs = [kT_ref[p * LANES:(p + 1) * LANES, blk(j)].astype(BF16) for p in range(2)]
        vTs = values(jnp.maximum(j - 1, 0))
        if mode == 'dsa':
            bias = lax.bitcast_convert_type(sk_ref[j], F32)
        for g in range(n_maps):
            qp, p = qps[g]
            hv = g // 2 if mode == 'diff' else g
            pv = _dot_nt(p_ref[g], vTs[hv])
            s = _dot(qp, kTs[p])
            if mode == 'fox':
                s = s - auxT_ref[2 * N_HEADS + g:2 * N_HEADS + g + 1, blk(j)] * LOG2E
            if mode == 'dsa':
                s = s + bias
            elif diag:
                s = jnp.where(causal, s, NEG)
            parts = [s[:, c * LANES:(c + 1) * LANES] for c in range(n_sub)]
            top = parts[0]
            for c in range(1, n_sub):
                top = jnp.maximum(top, parts[c])
            top = jnp.broadcast_to(jnp.max(top, axis=1, keepdims=True), (tb, LANES))
            m_old = m_ref[g]
            if mode == 'fox':
                m_new = jnp.maximum(m_old, top + cq_ref[g])
                shift = m_new - cq_ref[g]
            else:
                m_new = jnp.maximum(m_old, top)
                shift = m_new
            m_ref[g] = m_new
            p_ref[g] = jnp.concatenate([jnp.exp2(part - shift).astype(BF16) for part in parts], axis=1)
            acc_ref[g] = (acc_ref[g] + pv) * jnp.exp2(m_old - m_new)

    def full_step(j, carry):
        step(j, False)
        return carry

    lax.fori_loop(0, n_full, full_step, 0)
    step(n_full, True)
    vTs = values(n_full)
    results = []
    for g in range(n_maps):
        acc = acc_ref[g] + _dot_nt(p_ref[g], vTs[g // 2 if mode == 'diff' else g])
        results.append(acc[:, :DH] / acc[:, DH:DH + 1])

    if mode == 'diff':
        lam = _lam_value(lam_ref, lam_init)
        for h in range(N_HEADS):
            o = results[2 * h] - lam * results[2 * h + 1]
            o = _rms(o, gd_ref[:, h * DH:(h + 1) * DH]) * (1.0 - lam_init)
            o_ref[:, h * DH:(h + 1) * DH] = o.astype(o_ref.dtype)
    else:
        for h in range(N_HEADS):
            o_ref[:, h * DH:(h + 1) * DH] = results[h].astype(o_ref.dtype)


def _attn_prompt(mode, q, kT, vT, extra, *, batch, seq, tb, tk, lam_init=0.0):
    nq = seq // tb
    tile = lambda width: pl.BlockSpec((tb, width), lambda b, i: (b * nq + i, 0))
    resident = lambda rows: pl.BlockSpec((None, rows, seq), lambda b, i: (b, 0, 0))
    in_specs = [tile(256), resident(256), resident(256)]
    scratch = []
    if mode == 'fox':
        in_specs += [tile(LANES), resident(16)]
    elif mode == 'diff':
        in_specs += [pl.BlockSpec((4, DC), lambda b, i: (0, 0)), pl.BlockSpec((1, 256), lambda b, i: (0, 0))]
    else:
        in_specs += [tile(256), resident(DI), tile(LANES)]
        scratch = [pltpu.VMEM((seq // tk, tb, tk), I32), pltpu.VMEM((seq // tk, tb, tk), jnp.int16),
                   pltpu.VMEM((seq // tk, tb, tk), jnp.int16)]
    n_maps = 2 * N_HEADS if mode == 'diff' else N_HEADS
    if mode == 'fox':
        scratch = [pltpu.VMEM((n_maps, tb, LANES), F32)]
    scratch += [pltpu.VMEM((n_maps, tb, LANES), F32), pltpu.VMEM((n_maps, tb, LANES), F32),
                pltpu.VMEM((n_maps, tb, tk), BF16)]
    return pl.pallas_call(
        functools.partial(_attn_prompt_kernel, mode=mode, tb=tb, tk=tk, k_sel=min(TOPK_MAX, seq // 4),
                          lam_init=lam_init),
        grid=(batch, nq), in_specs=in_specs, out_specs=tile(256),
        out_shape=jax.ShapeDtypeStruct((batch * seq, 256), BF16), scratch_shapes=scratch,
        compiler_params=_cparams(("parallel", "arbitrary")), name=mode + "_prompt",
    )(q, kT, vT, *extra)


def _pad_rows(x, rows):
    return jnp.concatenate([x, jnp.zeros((rows - x.shape[0], x.shape[1]), x.dtype)], axis=0)


def _tile_rows(x, reps):
    return jnp.concatenate([x] * reps, axis=0)


def _prefix_rows(x):
    n = x.shape[0]
    rid = lax.broadcasted_iota(I32, (n, 1), 0)
    out = jnp.zeros_like(x)
    for u in range(n):
        out = out + jnp.where(rid >= u, x[u:u + 1, :], 0.0)
    return out


class _PageStream:
    def __init__(self, pt_ref, caches, bufs, sems, *, layer, group, n_pages, reverse):
        self.pt_ref, self.caches, self.bufs, self.sems = pt_ref, caches, bufs, sems
        self.layer, self.group, self.n_pages, self.reverse = layer, group, n_pages, reverse
        self.n_steps = n_pages // group

    def _copies(self, page, slot, g):
        return [pltpu.make_async_copy(c.at[self.layer, page], buf.at[slot, g], sem.at[slot, g])
                for c, buf, sem in zip(self.caches, self.bufs, self.sems)]

    def start(self, b, step, slot):
        for g in range(self.group):
            p = step * self.group + g
            if self.reverse:
                p = self.n_pages - 1 - p
            for cp in self._copies(self.pt_ref[b, p], slot, g):
                cp.start()

    def wait(self, slot):
        for g in range(self.group):
            for cp in self._copies(0, slot, g):
                cp.wait()

    def run(self, compute):
        b = pl.program_id(0)

        @pl.when(b == 0)
        def _():
            self.start(0, 0, 0)

        def body(s, carry):
            slot = lax.rem(s, 2)

            @pl.when(s + 1 < self.n_steps)
            def _():
                self.start(b, s + 1, 1 - slot)

            @pl.when((s + 1 == self.n_steps) & (b + 1 < pl.num_programs(0)))
            def _():
                self.start(b + 1, 0, 0)

            self.wait(slot)
            compute(s, slot)
            return carry

        lax.fori_loop(0, self.n_steps, body, 0)


def _attn_sample_kernel(pt_ref, *refs, mode, layer, group, n_pages, seq, lam_init):
    q_ref, kn_ref, vn_ref, kc_ref, vc_ref = refs[:5]
    rest = refs[5:]
    if mode == 'fox':
        (lfc_ref, combn_ref, o_ref, qx_ref, m_ref, l_ref, acc_ref, sfx_ref,
         kbuf, vbuf, lfbuf, ksem, vsem, lfsem) = rest
        stream = _PageStream(pt_ref, (kc_ref, vc_ref, lfc_ref), (kbuf, vbuf, lfbuf), (ksem, vsem, lfsem),
                             layer=layer, group=group, n_pages=n_pages, reverse=True)
    else:
        if mode == 'diff':
            lam_ref, gd_ref, o_ref, qx_ref, m_ref, l_ref, acc_ref, kbuf, vbuf, ksem, vsem = rest
        else:
            sel_ref, o_ref, qx_ref, m_ref, l_ref, acc_ref, kbuf, vbuf, ksem, vsem = rest
        stream = _PageStream(pt_ref, (kc_ref, vc_ref), (kbuf, vbuf), (ksem, vsem),
                             layer=layer, group=group, n_pages=n_pages, reverse=False)
    n_maps = 2 * N_HEADS if mode == 'diff' else N_HEADS
    w = 256 // n_maps
    rows = n_maps * seq

    q = q_ref[...]
    for g in range(n_maps):
        qx_ref[g * seq:(g + 1) * seq, :] = jnp.where(_lane_mask(256, g * w, (g + 1) * w), q, 0.0)
    m_ref[...] = jnp.full_like(m_ref, NEG)
    l_ref[...] = jnp.zeros_like(l_ref)
    acc_ref[...] = jnp.zeros_like(acc_ref)
    if mode == 'fox':
        sfx_ref[...] = jnp.zeros_like(sfx_ref)

    def new_prefix():
        cn = _prefix_rows(combn_ref[...])
        return cn, jnp.concatenate([cn[:, h:h + 1] for h in range(N_HEADS)], axis=0)

    def update(s_parts, mask, v_parts, nt):
        s = s_parts[0] if len(s_parts) == 1 else jnp.concatenate(s_parts, axis=1)
        pr, m2, alpha, l2 = _online(s, mask, m_ref[...], l_ref[...])
        m_ref[...] = m2
        l_ref[...] = l2
        pv = None
        for g, v in enumerate(v_parts):
            pg = pr[:, g * LANES:(g + 1) * LANES].astype(BF16)
            t = _dot_nt(pg, v) if nt else _dot(pg, v)
            pv = t if pv is None else pv + t
        acc_ref[...] = acc_ref[...] * alpha + pv

    def past_step(s_idx, slot):
        qx = qx_ref[...].astype(BF16)
        s_parts, v_parts = [], []
        if mode == 'fox':
            r = lax.broadcasted_iota(I32, (LANES, LANES), 0)
            c = lax.broadcasted_iota(I32, (LANES, LANES), 1)
            later = jnp.where(r > c, 1.0, 0.0).astype(BF16)
            _, cq = new_prefix()
            lf_rows = jnp.concatenate(
                [jnp.broadcast_to(lfbuf[slot, g, h:h + 1, :], (seq, LANES))
                 for g in range(group) for h in range(N_HEADS)], axis=0)
            a1, a2, a3 = _split3(lf_rows)
            within = _dot(a1, later) + _dot(a2, later) + _dot(a3, later)
            totals = jnp.sum(lf_rows, axis=1, keepdims=True)
            sfx = sfx_ref[...]
        for g in range(group):
            s = _dot(qx, kbuf[slot, g].astype(BF16))
            if mode == 'fox':
                s = s + (cq + (within[g * rows:(g + 1) * rows, :] + sfx)) * LOG2E
                sfx = sfx + totals[g * rows:(g + 1) * rows, :]
            if mode == 'dsa':
                s = s + _tile_rows(sel_ref[s_idx * group + g], n_maps)
            s_parts.append(s)
            v_parts.append(vbuf[slot, g].astype(BF16))
        if mode == 'fox':
            sfx_ref[...] = sfx
        update(s_parts, None, v_parts, True)

    stream.run(past_step)

    qx = qx_ref[...].astype(BF16)
    kpad = _pad_rows(kn_ref[...], LANES).astype(BF16)
    vpad = _pad_rows(vn_ref[...], LANES).astype(BF16)
    s = _dot_nt(qx, kpad)
    qi = lax.broadcasted_iota(I32, (rows, LANES), 0) % seq
    kj = lax.broadcasted_iota(I32, (rows, LANES), 1)
    mask = kj <= qi
    if mode == 'fox':
        cn, cq = new_prefix()
        cnT = _pad_rows(cn, LANES).T
        ck = jnp.concatenate([jnp.broadcast_to(cnT[h:h + 1, :], (seq, LANES)) for h in range(N_HEADS)], axis=0)
        s = s + (cq - ck) * LOG2E
    if mode == 'dsa':
        s = s + _tile_rows(sel_ref[n_pages], n_maps)
    update([s], mask, [vpad], False)
    res = acc_ref[...] / l_ref[...]
    out = jnp.zeros((seq, 256), F32)
    if mode == 'diff':
        lam = _lam_value(lam_ref, lam_init)
        for h in range(N_HEADS):
            o = res[2 * h * seq:(2 * h + 1) * seq, :] - lam * res[(2 * h + 1) * seq:(2 * h + 2) * seq, :]
            hm = _lane_mask(256, h * DH, (h + 1) * DH)
            ms = jnp.sum(jnp.where(hm, o * o, 0.0), axis=1, keepdims=True) * (1.0 / DH)
            o = o * lax.rsqrt(ms + EPS) * gd_ref[...] * (1.0 - lam_init)
            out = out + jnp.where(hm, o, 0.0)
    else:
        for h in range(N_HEADS):
            out = out + jnp.where(_lane_mask(256, h * DH, (h + 1) * DH), res[h * seq:(h + 1) * seq, :], 0.0)
    o_ref[...] = out.astype(o_ref.dtype)


def _page_scratch(n_caches_rows, group):
    bufs = [pltpu.VMEM((2, group, rows, PAGE_SIZE), F32) for rows in n_caches_rows]
    sems = [pltpu.SemaphoreType.DMA((2, group)) for _ in n_caches_rows]
    return bufs + sems


def _attn_sample(mode, page_table, q, kn, vn, cache_kT, cache_vT, extra, *, layer, batch, seq, group, lam_init=0.0):
    n_pages = page_table.shape[1]
    assert (n_pages // group) % 2 == 0 and n_pages % group == 0
    n_maps = 2 * N_HEADS if mode == 'diff' else N_HEADS
    rows = n_maps * seq
    per_b = lambda width: pl.BlockSpec((None, seq, width), lambda b, pt: (b, 0, 0))
    hbm = pl.BlockSpec(memory_space=pl.ANY)
    in_specs = [per_b(256), per_b(256), per_b(256), hbm, hbm]
    args = [q, kn, vn, cache_kT, cache_vT]
    scratch = [pltpu.VMEM((rows, 256), F32), pltpu.VMEM((rows, 1), F32), pltpu.VMEM((rows, 1), F32),
               pltpu.VMEM((rows, 256), F32)]
    if mode == 'fox':
        cache_lfT, comb_new = extra
        in_specs += [hbm, per_b(LANES)]
        args += [cache_lfT, comb_new]
        scratch += [pltpu.VMEM((rows, 1), F32)] + _page_scratch((256, 256, N_HEADS), group)
    elif mode == 'diff':
        lam4, gd = extra
        in_specs += [pl.BlockSpec((4, DC), lambda b, pt: (0, 0)), pl.BlockSpec((1, 256), lambda b, pt: (0, 0))]
        args += [lam4, gd]
        scratch += _page_scratch((256, 256), group)
    else:
        (sel,) = extra
        in_specs += [pl.BlockSpec((None, n_pages + 1, seq, LANES), lambda b, pt: (b, 0, 0, 0))]
        args += [sel]
        scratch += _page_scratch((256, 256), group)
    return pl.pallas_call(
        functools.partial(_attn_sample_kernel, mode=mode, layer=layer, group=group, n_pages=n_pages, seq=seq,
                          lam_init=lam_init),
        grid_spec=pltpu.PrefetchScalarGridSpec(
            num_scalar_prefetch=1, grid=(batch,), in_specs=in_specs,
            out_specs=pl.BlockSpec((None, seq, 256), lambda b, pt: (b, 0, 0)), scratch_shapes=scratch),
        out_shape=jax.ShapeDtypeStruct((batch, seq, 256), F32),
        compiler_params=_cparams(("arbitrary",)), name=mode + "_sample",
    )(page_table, *args)


def _dsa_select_sample_kernel(pt_ref, qi_ref, kin_ref, combn_ref, ikc_ref, sk_ref, qix_ref, ikbuf, iksem, *,
                              layer, group, n_pages, seq):
    stream = _PageStream(pt_ref, (ikc_ref,), (ikbuf,), (iksem,), layer=layer, group=group, n_pages=n_pages,
                         reverse=False)
    qi = qi_ref[...]
    for h in range(N_HEADS):
        qix_ref[h * seq:(h + 1) * seq, :] = _masked_pair(qi, h, DI)[0]

    def scores(logits):
        comb = combn_ref[...]
        sc = jnp.zeros((seq, LANES), F32)
        for h in range(N_HEADS):
            wcol = comb[:, N_HEADS + h:N_HEADS + h + 1] * (DI ** -0.5 * N_HEADS ** -0.5)
            sc = sc + jnp.maximum(logits[h * seq:(h + 1) * seq, :], 0.0) * wcol
        return sc

    def past_step(s_idx, slot):
        qix = qix_ref[...].astype(BF16)
        for g in range(group):
            ikT = ikbuf[slot, g].astype(BF16)
            sk_ref[s_idx * group + g] = _order_key(scores(_dot(qix, jnp.concatenate([ikT, ikT], axis=0))))

    stream.run(past_step)

    qix = qix_ref[...].astype(BF16)
    kin = kin_ref[...]
    kin2 = _pad_rows(kin + pltpu.roll(kin, DI, axis=1), LANES).astype(BF16)
    sc = scores(_dot_nt(qix, kin2))
    qrow = lax.broadcasted_iota(I32, (seq, LANES), 0)
    kcol = lax.broadcasted_iota(I32, (seq, LANES), 1)
    sk_ref[n_pages] = _order_key(jnp.where(kcol <= qrow, sc, -jnp.inf))


def _dsa_pick_sample_kernel(sk_ref, bias_ref, *, k_sel):
    nb, n_blk, seq, _ = sk_ref.shape

    def count_ge(cand):
        ind = jnp.where(sk_ref[...] >= cand, 1.0, 0.0)
        return jnp.sum(jnp.sum(ind, axis=1, keepdims=True), axis=3, keepdims=True)

    thr = _kth_largest_key(count_ge, float(k_sel), (nb, 1, seq, 1))
    keys = sk_ref[...]
    n_gt = jnp.sum(jnp.sum(jnp.where(keys > thr, 1.0, 0.0), axis=1, keepdims=True), axis=3, keepdims=True)
    need = (float(k_sel) - n_gt)[:, 0]
    eq = keys == thr
    eq2d = jnp.where(eq, 1.0, 0.0).reshape(nb * n_blk * seq, LANES).astype(BF16)
    within = _dot(eq2d, _upper_tri(LANES)).reshape(nb, n_blk, seq, LANES)
    seen = jnp.zeros((nb, seq, 1), F32)
    for j in range(n_blk):
        rank = seen + within[:, j]
        sel = (keys[:, j] > thr[:, 0]) | (eq[:, j] & (rank <= need))
        bias_ref[:, j] = jnp.where(sel, 0.0, NEG)
        seen = rank[:, :, LANES - 1:LANES]


def _dsa_select_sample(page_table, qi, ki_new, comb_new, cache_ikT, *, layer, batch, seq, group):
    n_pages = page_table.shape[1]
    assert (n_pages // group) % 2 == 0 and n_pages % group == 0
    per_b = lambda width: pl.BlockSpec((None, seq, width), lambda b, pt: (b, 0, 0))
    in_specs = [per_b(256), per_b(LANES), per_b(LANES), pl.BlockSpec(memory_space=pl.ANY)]
    k_sel = min(TOPK_MAX, (n_pages * PAGE_SIZE + seq) // 4)
    keys = pl.pallas_call(
        functools.partial(_dsa_select_sample_kernel, layer=layer, group=group, n_pages=n_pages, seq=seq),
        grid_spec=pltpu.PrefetchScalarGridSpec(
            num_scalar_prefetch=1, grid=(batch,), in_specs=in_specs,
            out_specs=pl.BlockSpec((None, n_pages + 1, seq, LANES), lambda b, pt: (b, 0, 0, 0)),
            scratch_shapes=[pltpu.VMEM((N_HEADS * seq, LANES), F32)] + _page_scratch((DI,), group)),
        out_shape=jax.ShapeDtypeStruct((batch, n_pages + 1, seq, LANES), I32),
        compiler_params=_cparams(("arbitrary",)), name="dsa_score_sample",
    )(page_table, qi, ki_new, comb_new, cache_ikT)
    shape = (batch, n_pages + 1, seq, LANES)
    whole = pl.BlockSpec(shape, lambda i: (0, 0, 0, 0))
    return pl.pallas_call(
        functools.partial(_dsa_pick_sample_kernel, k_sel=k_sel), grid=(1,), in_specs=[whole], out_specs=whole,
        out_shape=jax.ShapeDtypeStruct(shape, F32), compiler_params=_cparams(("arbitrary",)), name="dsa_pick_sample",
    )(keys)


def _mix_kernel(x_ref, oa_ref, ob_ref, oc_ref, od_ref, gpre_ref, wg_ref, bg_ref, wpa_ref, wpb_ref, wpc_ref,
                wpd_ref, wo_ref, gpost_ref, y_ref):
    x = x_ref[...]
    h = _rms(x, gpre_ref[...]).astype(BF16)
    mixed = None
    for n, (o_ref, wp_ref) in enumerate(((oa_ref, wpa_ref), (ob_ref, wpb_ref), (oc_ref, wpc_ref), (od_ref, wpd_ref))):
        gate = jax.nn.sigmoid(_dot(h, wg_ref[:, n * D_MODEL:(n + 1) * D_MODEL]) + bg_ref[:, n * D_MODEL:(n + 1) * D_MODEL])
        t = gate * _dot(o_ref[...].astype(BF16), wp_ref[...])
        mixed = t if mixed is None else mixed + t
    y = _dot(mixed.astype(BF16), wo_ref[...])
    y_ref[...] = x + _rms(y, gpost_ref[...])


def _mix(x2d, oa, ob, oc, od, gpre, wg, bg, wpa, wpb, wpc, wpd, wo, gpost, *, tm):
    n = x2d.shape[0]
    tile = lambda width: pl.BlockSpec((tm, width), lambda i: (i, 0))
    const = lambda a: pl.BlockSpec(a.shape, lambda i: (0, 0))
    consts = (gpre, wg, bg, wpa, wpb, wpc, wpd, wo, gpost)
    return pl.pallas_call(
        _mix_kernel, grid=(n // tm,),
        in_specs=[tile(D_MODEL), tile(512), tile(256), tile(256), tile(256)] + [const(a) for a in consts],
        out_specs=tile(D_MODEL), out_shape=jax.ShapeDtypeStruct((n, D_MODEL), F32),
        compiler_params=_cparams(("parallel",)), name="mix",
    )(x2d, oa, ob, oc, od, *consts)


def _ffn_kernel(x_ref, gpre_ref, wgu_ref, wd_ref, gpost_ref, y_ref):
    x = x_ref[...]
    h = _rms(x, gpre_ref[...]).astype(BF16)
    g = _dot(h, wgu_ref[:, :D_FF])
    u = _dot(h, wgu_ref[:, D_FF:])
    a = (g * jax.nn.sigmoid(g) * u).astype(BF16)
    y_ref[...] = x + _rms(_dot(a, wd_ref[...]), gpost_ref[...])


def _ffn(x2d, gpre, wgu, wd, gpost, *, tm):
    n = x2d.shape[0]
    tile = pl.BlockSpec((tm, D_MODEL), lambda i: (i, 0))
    const = lambda a: pl.BlockSpec(a.shape, lambda i: (0, 0))
    consts = (gpre, wgu, wd, gpost)
    return pl.pallas_call(
        _ffn_kernel, grid=(n // tm,), in_specs=[tile] + [const(a) for a in consts],
        out_specs=tile, out_shape=jax.ShapeDtypeStruct((n, D_MODEL), F32),
        compiler_params=_cparams(("parallel",)), name="ffn",
    )(x2d, *consts)


def _rope_tables(pos):
    def one(headw, rot, theta):
        half = rot // 2
        inv = theta ** (-jnp.arange(half, dtype=F32) / half)
        ang = pos.astype(F32)[:, None] * inv[None, :]
        cos, sin = jnp.cos(ang), jnp.sin(ang)
        rest = headw - rot
        n = pos.shape[0]
        c = jnp.concatenate([cos, cos, jnp.ones((n, rest), F32)], axis=1)
        s = jnp.concatenate([-sin, sin, jnp.zeros((n, rest), F32)], axis=1)
        return jnp.tile(c, (1, 256 // headw)), jnp.tile(s, (1, 256 // headw))
    cr, sr = one(DKA, DKA, RET_THETA)
    c32, s32 = one(DC, DC // ROPE_FRAC, ROPE_THETA)
    c64, s64 = one(DH, DH // ROPE_FRAC, ROPE_THETA)
    return (cr, sr, c32, s32, c64, s64)


def _pack_w_in(w_in):
    depth = w_in.shape[0]
    seg = lambda name: w_in[:, :, _SRC[name][0]:_SRC[name][0] + _SRC[name][1]]
    zeros = lambda width: jnp.zeros((depth, D_MODEL, width), w_in.dtype)
    parts = [seg(n) for n in _PACK_ORDER]
    parts += [seg('ki'), zeros(LANES - DI), seg('fb'), seg('wi'), zeros(LANES - 2 * N_HEADS)]
    return jnp.concatenate(parts, axis=2).astype(BF16)


def _heads_T(a, batch, seq):
    return a.reshape(batch, N_HEADS, a.shape[1] // N_HEADS, seq).transpose(0, 3, 1, 2)


def kernel(x_prompt, x_sample, state_ret, cache_fox_k, cache_fox_v, cache_fox_lf, cache_diff_k, cache_diff_v, cache_dsa_k, cache_dsa_v, cache_dsa_ik, page_table, g_pre_mix, g_post_mix, g_pre_ffn, g_post_ffn, w_in, b_forget, lam_q1, lam_k1, lam_q2, lam_k2, g_diff, w_pa, w_pb, w_pc, w_pd, w_gate, b_gate, w_out, w_gu, w_down):
    bp, tp, _ = x_prompt.shape
    bs, ts, _ = x_sample.shape
    depth = w_in.shape[0]
    n_pool = cache_fox_k.shape[1]
    past = page_table.shape[1] * PAGE_SIZE
    tm_p = min(256, tp)
    tb = min(256, tp)
    tk = min(256, tp)
    group = min(16, page_table.shape[1] // 2)

    w_pack = _pack_w_in(w_in)
    bf_pad = jnp.pad(b_forget, ((0, 0), (0, LANES - N_HEADS)))[:, None, :]
    tabs_p = _rope_tables(jnp.arange(tp, dtype=I32))
    tabs_s = _rope_tables(jnp.tile(past + jnp.arange(ts, dtype=I32), bs))
    row = lambda a, l: a[l][None, :]
    bf = lambda a: a.astype(BF16)
    wg16, wpa16, wpb16, wpc16, wpd16, wo16, wgu16, wd16 = map(bf, (w_gate, w_pa, w_pb, w_pc, w_pd, w_out, w_gu, w_down))
    lam4 = jnp.stack([lam_q1, lam_k1, lam_q2, lam_k2], axis=1)
    pageT = lambda c: jnp.moveaxis(c.reshape(depth, n_pool, PAGE_SIZE, -1), 2, 3)
    fkT, fvT, ckT, cvT, dkT, dvT, ikT, lfT = map(pageT, (cache_fox_k, cache_fox_v, cache_diff_k, cache_diff_v,
                                                         cache_dsa_k, cache_dsa_v, cache_dsa_ik, cache_fox_lf))

    xp = x_prompt.reshape(bp * tp, D_MODEL)
    xs = x_sample.reshape(bs * ts, D_MODEL)
    new = {n: [] for n in ('ret_p', 'ret_s', 'fk_p', 'fv_p', 'flf_p', 'fk_s', 'fv_s', 'flf_s', 'ck_p', 'cv_p',
                           'ck_s', 'cv_s', 'dk_p', 'dv_p', 'dik_p', 'dk_s', 'dv_s', 'dik_s')}
    for l in range(depth):
        lam_init = 0.8 - 0.6 * math.exp(-0.3 * l)
        finish = lambda x, oa, ob, oc, od, tm: _ffn(
            _mix(x, oa, ob, oc, od, row(g_pre_mix, l), wg16[l], row(b_gate, l), wpa16[l], wpb16[l], wpc16[l],
                 wpd16[l], wo16[l], row(g_post_mix, l), tm=tm),
            row(g_pre_ffn, l), wgu16[l], wd16[l], row(g_post_ffn, l), tm=tm)
        pr = _proj(xp, row(g_pre_mix, l), w_pack[l], bf_pad[l], tabs_p, mode='prompt', batch=bp, seq=tp, tm=tm_p)
        oa, r_p = _ret_prompt(pr['qa'], pr['ka'], pr['va'], pr['ga'], batch=bp, seq=tp)
        tiles = dict(batch=bp, seq=tp, tb=tb, tk=tk)
        ob = _attn_prompt('fox', pr['qb'], pr['kb'], pr['vb'], (pr['comb'], pr['auxT']), **tiles)
        oc = _attn_prompt('diff', pr['qc'], pr['kc'], pr['vc'], (lam4[l], row(g_diff, l)), lam_init=lam_init, **tiles)
        od = _attn_prompt('dsa', pr['qd'], pr['kd'], pr['vd'], (pr['qi'], pr['ki'], pr['comb']), **tiles)
        xp = finish(xp, oa, ob, oc, od, tm_p)
        new['ret_p'].append(r_p.reshape(bp, N_HEADS, DKA, DVA))
        for name, key in (('fk_p', 'kb'), ('fv_p', 'vb'), ('ck_p', 'kc'), ('cv_p', 'vc'), ('dk_p', 'kd'), ('dv_p', 'vd')):
            new[name].append(_heads_T(pr[key], bp, tp))
        new['flf_p'].append(jnp.swapaxes(pr['auxT'][:, :N_HEADS, :], 1, 2))
        new['dik_p'].append(jnp.swapaxes(pr['ki'], 1, 2))
        sm = _proj(xs, row(g_pre_mix, l), w_pack[l], bf_pad[l], tabs_s, mode='sample', batch=bs, seq=ts, tm=bs * ts)
        oa, r_s = _ret_sample(sm['qa'], sm['ka'], sm['va'], sm['ga'], state_ret[l].reshape(bs, 256, DVA), batch=bs, seq=ts)
        per_b = lambda a: a.reshape(bs, ts, a.shape[-1])
        common = dict(layer=l, batch=bs, seq=ts, group=group)
        ob = _attn_sample('fox', page_table, per_b(sm['qb']), per_b(sm['kb']), per_b(sm['vb']), fkT, fvT,
                          (lfT, per_b(sm['comb'])), **common)
        oc = _attn_sample('diff', page_table, per_b(sm['qc']), per_b(sm['kc']), per_b(sm['vc']), ckT, cvT,
                          (lam4[l], row(g_diff, l)), lam_init=lam_init, **common)
        sel = _dsa_select_sample(page_table, per_b(sm['qi']), per_b(sm['ki']), per_b(sm['comb']), ikT, **common)
        od = _attn_sample('dsa', page_table, per_b(sm['qd']), per_b(sm['kd']), per_b(sm['vd']), dkT, dvT, (sel,), **common)
        flat = lambda a: a.reshape(bs * ts, 256)
        xs = finish(xs, oa, flat(ob), flat(oc), flat(od), bs * ts)
        new['ret_s'].append(r_s.reshape(bs, N_HEADS, DKA, DVA))
        for name, key in (('fk_s', 'kb'), ('fv_s', 'vb'), ('ck_s', 'kc'), ('cv_s', 'vc'), ('dk_s', 'kd'), ('dv_s', 'vd')):
            new[name].append(sm[key].reshape(bs, ts, N_HEADS, DH))
        new['flf_s'].append(sm['comb'][:, :N_HEADS].reshape(bs, ts, N_HEADS))
        new['dik_s'].append(sm['ki'][:, :DI].reshape(bs, ts, DI))
    st = lambda n: jnp.stack(new[n], axis=0)
    return (xp.reshape(bp, tp, D_MODEL), xs.reshape(bs, ts, D_MODEL), st('ret_p'), st('ret_s'),
            st('fk_p'), st('fv_p'), st('flf_p'), st('fk_s'), st('fv_s'), st('flf_s'),
            st('ck_p'), st('cv_p'), st('ck_s'), st('cv_s'),
            st('dk_p'), st('dv_p'), st('dik_p'), st('dk_s'), st('dv_s'), st('dik_s'))
```

```python
import functools
import math

import jax
import jax.numpy as jnp
from jax import lax
from jax.experimental import pallas as pl
from jax.experimental.pallas import tpu as pltpu

F32 = jnp.float32
BF16 = jnp.bfloat16
I32 = jnp.int32

D_MODEL = 1024
HEAD_DIM = 64
N_HEADS = 4
DKA = HEAD_DIM
DVA = 2 * HEAD_DIM
RET_CHUNK = 128
RET_THETA = 10000.0
DH = HEAD_DIM
DC = HEAD_DIM // 2
DI = HEAD_DIM
TOPK_MAX = 256
ROPE_THETA = 500000.0
ROPE_FRAC = 4
D_FF = 2816
EPS = 1e-6
PAGE_SIZE = 128
LANES = 128
NEG = -1e30
INT_MIN = -2 ** 31
LOG2E = math.log2(math.e)
VMEM_LIMIT = 56 * 1024 * 1024

LOG_GAMMA = tuple(math.log1p(-(2.0 ** (-5.0 - h))) for h in range(N_HEADS))

_SRC = dict(qa=(0, 256), ka=(256, 256), va=(512, 512), ga=(1024, 512),
            qb=(1536, 256), kb=(1792, 256), vb=(2048, 256), fb=(2304, 4),
            qc=(2308, 256), kc=(2564, 256), vc=(2820, 256),
            qd=(3076, 256), kd=(3332, 256), vd=(3588, 256),
            qi=(3844, 256), ki=(4100, 64), wi=(4164, 4))
_PACK_ORDER = ('qa', 'ka', 'va', 'ga', 'qb', 'kb', 'vb', 'qc', 'kc', 'vc', 'qd', 'kd', 'vd', 'qi')
_GRP = {}
_off = 0
for _n in _PACK_ORDER:
    _GRP[_n] = (_off, _SRC[_n][1])
    _off += _SRC[_n][1]
_GRP['ki'] = (_off, LANES)
_off += LANES
_GRP['aux'] = (_off, LANES)
_off += LANES
N_PACK = _off


def _cparams(sem):
    return pltpu.CompilerParams(dimension_semantics=sem, vmem_limit_bytes=VMEM_LIMIT)


def _rms(x, g=None):
    y = x * lax.rsqrt(jnp.mean(x * x, axis=-1, keepdims=True) + EPS)
    return y if g is None else y * g


def _dot(a, b):
    return jnp.dot(a, b, preferred_element_type=F32)


def _dot_nt(a, b):
    return lax.dot_general(a, b, (((1,), (1,)), ((), ())), preferred_element_type=F32)


def _split3(x):
    x1 = x.astype(BF16)
    r1 = x - x1.astype(F32)
    x2 = r1.astype(BF16)
    x3 = (r1 - x2.astype(F32)).astype(BF16)
    return x1, x2, x3


def _lane_mask(width, lo, hi):
    lane = lax.broadcasted_iota(I32, (1, width), 1)
    return (lane >= lo) & (lane < hi)


def _masked_pair(q, g, w):
    p = (g * w) // LANES
    lo = g * w - p * LANES
    qp = q[:, p * LANES:(p + 1) * LANES]
    return jnp.where(_lane_mask(LANES, lo, lo + w), qp, jnp.zeros_like(qp)), p


def _order_key(score):
    bits = lax.bitcast_convert_type(score + 0.0, I32)
    return jnp.where(bits < 0, bits ^ jnp.int32(0x7FFFFFFF), bits)


def _kth_largest_key(count_ge, k, shape):
    nonneg = count_ge(jnp.zeros(shape, I32)) >= k
    lo0 = jnp.where(nonneg, jnp.int32(0), jnp.int32(INT_MIN))

    def body(b, lo):
        cand = lo + jnp.left_shift(jnp.int32(1), 30 - b)
        return jnp.where(count_ge(cand) >= k, cand, lo)

    return lax.fori_loop(0, 31, body, lo0)


_PROMPT_OUTS = (('qa', 'tok', BF16), ('ka', 'T', BF16), ('va', 'tok', BF16), ('ga', 'tok', F32),
                ('qb', 'tok', BF16), ('kb', 'T', F32), ('vb', 'T', F32),
                ('qc', 'tok', BF16), ('kc', 'T', F32), ('vc', 'T', F32),
                ('qd', 'tok', BF16), ('kd', 'T', F32), ('vd', 'T', F32),
                ('qi', 'tok', BF16), ('ki', 'T', F32), ('comb', 'tok', F32), ('auxT', 'T', F32))
_SAMPLE_OUTS = (('qa', 'tok', F32), ('ka', 'T', F32), ('va', 'tok', F32), ('ga', 'tok', F32),
                ('qb', 'tok', F32), ('kb', 'tok', F32), ('vb', 'tok', F32),
                ('qc', 'tok', F32), ('kc', 'tok', F32), ('vc', 'tok', F32),
                ('qd', 'tok', F32), ('kd', 'tok', F32), ('vd', 'tok', F32),
                ('qi', 'tok', F32), ('ki', 'tok', F32), ('comb', 'tok', F32))
_ROPE_KIND = dict(qa='ret', ka='ret', qc='r32', kc='r32', qd='r64', kd='r64', qi='r64', ki='r64')
_SCALE = dict(ka=DKA ** -0.5, qb=DH ** -0.5 * LOG2E, qc=DC ** -0.5 * LOG2E, qd=DH ** -0.5 * LOG2E)
_T_ROWS = dict(ki=DI, auxT=16)


def _proj_kernel(*refs, outs, with_cumsum, tiles_per_seq):
    (x_ref, g_ref, w_ref, bf_ref, cr_ref, sr_ref, c32_ref, s32_ref, c64_ref, s64_ref) = refs[:10]
    out_refs = dict(zip([o[0] for o in outs], refs[10:10 + len(outs)]))
    tm = x_ref.shape[0]
    h = _rms(x_ref[...], g_ref[...]).astype(BF16)
    tables = dict(ret=(cr_ref, sr_ref, DKA, DKA // 2),
                  r32=(c32_ref, s32_ref, DC, DC // ROPE_FRAC // 2),
                  r64=(c64_ref, s64_ref, DH, DH // ROPE_FRAC // 2))

    def group(name):
        start, width = _GRP[name]
        p = _dot(h, w_ref[:, start:start + width])
        kind = _ROPE_KIND.get(name)
        if kind is not None:
            c_ref, s_ref, headw, half = tables[kind]
            lane = lax.broadcasted_iota(I32, (1, width), 1)
            first = (lane & (headw - 1)) < half
            rx = jnp.where(first, pltpu.roll(p, width - half, axis=1), pltpu.roll(p, half, axis=1))
            p = p * c_ref[:, :width] + rx * s_ref[:, :width]
        if name in _SCALE:
            p = p * _SCALE[name]
        return p

    comb = None
    for name, layout, dtype in outs:
        ref = out_refs[name]
        if name in ('comb', 'auxT'):
            if comb is None:
                a = group('aux')
                z = a + bf_ref[...]
                lf = jnp.minimum(z, 0.0) - jnp.log1p(jnp.exp(-jnp.abs(z)))
                lane = lax.broadcasted_iota(I32, (1, LANES), 1)
                comb = jnp.where(lane < N_HEADS, lf, a)
                if with_cumsum:
                    carry_ref = refs[-1]

                    @pl.when(pl.program_id(0) % tiles_per_seq == 0)
                    def _():
                        carry_ref[...] = jnp.zeros_like(carry_ref)

                    r = lax.broadcasted_iota(I32, (tm, tm), 0)
                    c = lax.broadcasted_iota(I32, (tm, tm), 1)
                    tri = jnp.where(c <= r, 1.0, 0.0).astype(BF16)
                    l1, l2, l3 = _split3(lf)
                    cs = _dot(tri, l1) + _dot(tri, l2) + _dot(tri, l3) + carry_ref[...]
                    carry_ref[...] = cs[tm - 1:tm, :]
                    comb = jnp.where(lane < 2 * N_HEADS, comb,
                                     jnp.where(lane < 3 * N_HEADS, pltpu.roll(cs, 2 * N_HEADS, axis=1), 0.0))
            val = comb
        else:
            val = group(name)
        if layout == 'tok':
            ref[...] = val.astype(dtype)
        else:
            vt = val.T
            rows = _T_ROWS.get(name, vt.shape[0])
            ref[...] = vt[:rows, :].astype(dtype)


def _proj(x2d, gain, w_pack, bf_pad, tabs, *, mode, batch, seq, tm):
    n = x2d.shape[0]
    outs = _PROMPT_OUTS if mode == 'prompt' else _SAMPLE_OUTS
    tps = seq // tm if mode == 'prompt' else 1
    n_tiles = n // tm
    tab_rows = tabs[0].shape[0]
    tab_tiles = tab_rows // tm
    in_specs = [pl.BlockSpec((tm, D_MODEL), lambda i: (i, 0)),
                pl.BlockSpec((1, D_MODEL), lambda i: (0, 0)),
                pl.BlockSpec((D_MODEL, N_PACK), lambda i: (0, 0)),
                pl.BlockSpec((1, LANES), lambda i: (0, 0))]
    in_specs += [pl.BlockSpec((tm, 256), lambda i: (i % tab_tiles, 0)) for _ in range(6)]
    out_specs, out_shapes = [], []
    for name, layout, dtype in outs:
        width = LANES if name in ('comb', 'auxT', 'ki') else _GRP[name][1]
        if layout == 'tok':
            out_specs.append(pl.BlockSpec((tm, width), lambda i: (i, 0)))
            out_shapes.append(jax.ShapeDtypeStruct((n, width), dtype))
        else:
            rows = _T_ROWS.get(name, width)
            if mode == 'prompt':
                out_specs.append(pl.BlockSpec((None, rows, tm), lambda i: (i // tps, 0, i % tps)))
                out_shapes.append(jax.ShapeDtypeStruct((batch, rows, seq), dtype))
            else:
                out_specs.append(pl.BlockSpec((rows, tm), lambda i: (0, i)))
                out_shapes.append(jax.ShapeDtypeStruct((rows, n), dtype))
    res = pl.pallas_call(
        functools.partial(_proj_kernel, outs=outs, with_cumsum=(mode == 'prompt'), tiles_per_seq=tps),
        grid=(n_tiles,), in_specs=in_specs, out_specs=out_specs, out_shape=out_shapes,
        scratch_shapes=[pltpu.VMEM((1, LANES), F32)],
        compiler_params=_cparams(("arbitrary",)), name="proj_" + mode,
    )(x2d, gain, w_pack, bf_pad, *tabs)
    return dict(zip([o[0] for o in outs], res))


def _ret_finish(o, ga_h):
    return _rms(o) * (ga_h * jax.nn.sigmoid(ga_h))


def _ret_prompt_kernel(q_ref, kT_ref, v_ref, ga_ref, o_ref, st_ref, r_ref):
    c = q_ref.shape[0]
    j = pl.program_id(1)

    @pl.when(j == 0)
    def _():
        r_ref[...] = jnp.zeros_like(r_ref)

    q = q_ref[...]
    row = lax.broadcasted_iota(I32, (c, c), 0)
    col = lax.broadcasted_iota(I32, (c, c), 1)
    rel = (row - col).astype(F32)
    irow = lax.broadcasted_iota(I32, (c, 1), 0).astype(F32)
    icol = lax.broadcasted_iota(I32, (1, c), 1).astype(F32)
    for h in range(N_HEADS):
        lg = LOG_GAMMA[h]
        dmask = jnp.where(rel >= 0, jnp.exp(jnp.maximum(rel, 0.0) * lg), 0.0)
        xi = jnp.exp((irow + 1.0) * lg)
        zeta = jnp.exp((c - 1.0 - icol) * lg)
        qp, p = _masked_pair(q, h, DKA)
        s = _dot(qp, kT_ref[p * LANES:(p + 1) * LANES, :]) * dmask
        v_h = v_ref[:, h * DVA:(h + 1) * DVA]
        r_pair = r_ref[p * LANES:(p + 1) * LANES, :].astype(BF16)
        o = _dot(s.astype(BF16), v_h) + _dot((qp.astype(F32) * xi).astype(BF16), r_pair)
        kz = (kT_ref[h * DKA:(h + 1) * DKA, :].astype(F32) * zeta).astype(BF16)
        r_ref[h * DKA:(h + 1) * DKA, :] = math.exp(c * lg) * r_ref[h * DKA:(h + 1) * DKA, :] + _dot(kz, v_h)
        o_ref[:, h * DVA:(h + 1) * DVA] = _ret_finish(o, ga_ref[:, h * DVA:(h + 1) * DVA]).astype(o_ref.dtype)

    @pl.when(j == pl.num_programs(1) - 1)
    def _():
        st_ref[...] = r_ref[...]


def _ret_prompt(qa, kaT, va, ga, *, batch, seq):
    c = RET_CHUNK
    nc = seq // c
    return pl.pallas_call(
        _ret_prompt_kernel, grid=(batch, nc),
        in_specs=[pl.BlockSpec((c, 256), lambda b, j: (b * nc + j, 0)),
                  pl.BlockSpec((None, 256, c), lambda b, j: (b, 0, j)),
                  pl.BlockSpec((c, 512), lambda b, j: (b * nc + j, 0)),
                  pl.BlockSpec((c, 512), lambda b, j: (b * nc + j, 0))],
        out_specs=[pl.BlockSpec((c, 512), lambda b, j: (b * nc + j, 0)),
                   pl.BlockSpec((None, 256, DVA), lambda b, j: (b, 0, 0))],
        out_shape=[jax.ShapeDtypeStruct((batch * seq, 512), BF16),
                   jax.ShapeDtypeStruct((batch, 256, DVA), F32)],
        scratch_shapes=[pltpu.VMEM((256, DVA), F32)],
        compiler_params=_cparams(("parallel", "arbitrary")), name="ret_prompt",
    )(qa, kaT, va, ga)


def _ret_sample_kernel(q_ref, kT_ref, v_ref, ga_ref, st_ref, o_ref, stn_ref, *, batch, seq):
    n = batch * seq
    q = q_ref[...]
    row = lax.broadcasted_iota(I32, (n, n), 0)
    col = lax.broadcasted_iota(I32, (n, n), 1)
    same = (row // seq) == (col // seq)
    rel = ((row % seq) - (col % seq)).astype(F32)
    irow = (lax.broadcasted_iota(I32, (n, 1), 0) % seq).astype(F32)
    icol_i = lax.broadcasted_iota(I32, (1, n), 1)
    icol = (icol_i % seq).astype(F32)
    seq_lane = lax.broadcasted_iota(I32, (1, batch * LANES), 1) // LANES
    tok_row = lax.broadcasted_iota(I32, (n, 1), 0) // seq
    own_state = seq_lane == tok_row
    seq_sub = lax.broadcasted_iota(I32, (batch * DKA, 1), 0) // DKA
    own_tok = seq_sub == (icol_i // seq)
    for h in range(N_HEADS):
        lg = LOG_GAMMA[h]
        dmask = jnp.where(same & (rel >= 0), jnp.exp(jnp.maximum(rel, 0.0) * lg), 0.0)
        xi = jnp.exp((irow + 1.0) * lg)
        zeta = jnp.exp((seq - 1.0 - icol) * lg)
        qp, p = _masked_pair(q, h, DKA)
        kT_pair = kT_ref[p * LANES:(p + 1) * LANES, :].astype(BF16)
        s = _dot(qp.astype(BF16), kT_pair) * dmask
        v_h = v_ref[:, h * DVA:(h + 1) * DVA].astype(BF16)
        qx = (qp * xi).astype(BF16)
        q_big = jnp.where(own_state, jnp.concatenate([qx] * batch, axis=1), jnp.zeros((), BF16))
        r_pair = st_ref[:, p * LANES:(p + 1) * LANES, :].reshape(batch * LANES, DVA).astype(BF16)
        o = _dot(s.astype(BF16), v_h) + _dot(q_big, r_pair)
        o_ref[:, h * DVA:(h + 1) * DVA] = _ret_finish(o, ga_ref[:, h * DVA:(h + 1) * DVA]).astype(o_ref.dtype)
        kz = (kT_ref[h * DKA:(h + 1) * DKA, :] * zeta).astype(BF16)
        k_big = jnp.where(own_tok, jnp.concatenate([kz] * batch, axis=0), jnp.zeros((), BF16))
        upd = _dot(k_big, v_h).reshape(batch, DKA, DVA)
        stn_ref[:, h * DKA:(h + 1) * DKA, :] = math.exp(seq * lg) * st_ref[:, h * DKA:(h + 1) * DKA, :] + upd


def _ret_sample(qa, kaT, va, ga, state, *, batch, seq):
    n = batch * seq
    full = lambda shape: pl.BlockSpec(shape, lambda i: (0,) * len(shape))
    return pl.pallas_call(
        functools.partial(_ret_sample_kernel, batch=batch, seq=seq), grid=(1,),
        in_specs=[full((n, 256)), full((256, n)), full((n, 512)), full((n, 512)), full((batch, 256, DVA))],
        out_specs=[full((n, 512)), full((batch, 256, DVA))],
        out_shape=[jax.ShapeDtypeStruct((n, 512), BF16), jax.ShapeDtypeStruct((batch, 256, DVA), F32)],
        compiler_params=_cparams(("arbitrary",)), name="ret_sample",
    )(qa, kaT, va, ga, state)


def _lam_value(lam_ref, lam_init):
    l4 = lam_ref[...]
    s1 = jnp.sum(l4[0:1, :] * l4[1:2, :], axis=1, keepdims=True)
    s2 = jnp.sum(l4[2:3, :] * l4[3:4, :], axis=1, keepdims=True)
    return jnp.exp(s1) - jnp.exp(s2) + lam_init


def _online(s, mask, m, l):
    if mask is not None:
        s = jnp.where(mask, s, NEG)
    m2 = jnp.maximum(m, jnp.max(s, axis=1, keepdims=True))
    alpha = jnp.exp2(m - m2)
    p = jnp.exp2(s - m2)
    if mask is not None:
        p = jnp.where(mask, p, 0.0)
    return p, m2, alpha, l * alpha + jnp.sum(p, axis=1, keepdims=True)


def _upper_tri(n):
    r = lax.broadcasted_iota(I32, (n, n), 0)
    c = lax.broadcasted_iota(I32, (n, n), 1)
    return jnp.where(r <= c, 1.0, 0.0).astype(BF16)


def _attn_prompt_kernel(*refs, mode, tb, tk, k_sel, lam_init):
    i = pl.program_id(1)
    n_full = (i * tb) // tk
    m_ref, acc_ref, p_ref = refs[-3:]
    refs = refs[:-3]
    if mode == 'fox':
        q_ref, kT_ref, vT_ref, comb_ref, auxT_ref, o_ref, cq_ref = refs
    elif mode == 'diff':
        q_ref, kT_ref, vT_ref, lam_ref, gd_ref, o_ref = refs
    else:
        q_ref, kT_ref, vT_ref, qi_ref, kiT_ref, comb_ref, o_ref, sk_ref = refs
    q = q_ref[...]
    row = lax.broadcasted_iota(I32, (tb, tk), 0)
    col = lax.broadcasted_iota(I32, (tb, tk), 1)
    causal = (n_full * tk + col) <= (i * tb + row)

    def blk(j):
        return pl.ds(pl.multiple_of(j * tk, tk), tk)

    if mode == 'dsa':
        qi = qi_ref[...]
        comb = comb_ref[...]
        qips = [_masked_pair(qi, h, DI)[0] for h in range(N_HEADS)]
        wcols = [comb[:, N_HEADS + h:N_HEADS + h + 1] * (DI ** -0.5 * N_HEADS ** -0.5) for h in range(N_HEADS)]

        def score_block(j, diag):
            kiT = kiT_ref[:, blk(j)].astype(BF16)
            kiT2 = jnp.concatenate([kiT, kiT], axis=0)
            sc = jnp.zeros((tb, tk), F32)
            for h in range(N_HEADS):
                sc = sc + jnp.maximum(_dot(qips[h], kiT2), 0.0) * wcols[h]
            if diag:
                sc = jnp.where(causal, sc, -jnp.inf)
            sk_ref[j] = _order_key(sc)

        def score_body(j, carry):
            score_block(j, False)
            return carry

        lax.fori_loop(0, n_full, score_body, 0)
        score_block(n_full, True)

        def count_ge(cand):
            def body(j, acc):
                ind = jnp.where(sk_ref[j] >= cand, 1.0, 0.0)
                for c in range(tk // LANES):
                    acc = acc + ind[:, c * LANES:(c + 1) * LANES]
                return acc
            acc = lax.fori_loop(0, n_full + 1, body, jnp.zeros((tb, LANES), F32))
            return jnp.sum(acc, axis=1, keepdims=True)

        thr = _kth_largest_key(count_ge, float(k_sel), (tb, 1))

        def gt_body(j, acc):
            ind = jnp.where(sk_ref[j] > thr, 1.0, 0.0)
            for c in range(tk // LANES):
                acc = acc + ind[:, c * LANES:(c + 1) * LANES]
            return acc
        n_gt = jnp.sum(lax.fori_loop(0, n_full + 1, gt_body, jnp.zeros((tb, LANES), F32)), axis=1, keepdims=True)
        need = float(k_sel) - n_gt
        tw = min(tk, 256)
        tri = _upper_tri(tw)

        def sel_block(j, seen, diag):
            x = sk_ref[j]
            eq = x == thr
            eqf = jnp.where(eq, 1.0, 0.0).astype(BF16)
            ranks = []
            for c in range(tk // tw):
                ranks.append(seen + _dot(eqf[:, c * tw:(c + 1) * tw], tri))
                seen = ranks[-1][:, tw - 1:tw]
            rank = ranks[0] if len(ranks) == 1 else jnp.concatenate(ranks, axis=1)
            sel = (x > thr) | (eq & (rank <= need))
            if diag:
                sel = sel & causal
            sk_ref[j] = lax.bitcast_convert_type(jnp.where(sel, 0.0, NEG), I32)
            return seen

        seen = lax.fori_loop(0, n_full, lambda j, s: sel_block(j, s, False), jnp.zeros((tb, 1), F32))
        sel_block(n_full, seen, True)

    n_maps = 2 * N_HEADS if mode == 'diff' else N_HEADS
    w = 256 // n_maps
    qps = [_masked_pair(q, g, w) for g in range(n_maps)]
    if mode == 'fox':
        for g in range(n_maps):
            cq_ref[g] = jnp.broadcast_to(comb_ref[:, 2 * N_HEADS + g:2 * N_HEADS + g + 1] * LOG2E, (tb, LANES))
    m_ref[...] = jnp.full_like(m_ref, NEG)
    acc_ref[...] = jnp.zeros_like(acc_ref)
    ones_row = jnp.where(lax.broadcasted_iota(I32, (LANES - DH, tk), 0) == 0, 1.0, 0.0).astype(BF16)
    n_sub = tk // LANES

    def values(j):
        return [jnp.concatenate([vT_ref[h * DH:(h + 1) * DH, blk(j)].astype(BF16), ones_row], axis=0)
                for h in range(N_HEADS)]

    p_ref[...] = jnp.zeros_like(p_ref)

    def step(j, diag):
        kTs = [kT_ref[p * LANES:(p + 1) * LANES, blk(j)].astype(BF16) for p in range(2)]
        vTs = values(jnp.maximum(j - 1, 0))
        if mode == 'dsa':
            bias = lax.bitcast_convert_type(sk_ref[j], F32)
        for g in range(n_maps):
            qp, p = qps[g]
            hv = g // 2 if mode == 'diff' else g
            pv = _dot_nt(p_ref[g], vTs[hv])
            s = _dot(qp, kTs[p])
            if mode == 'fox':
                s = s - auxT_ref[2 * N_HEADS + g:2 * N_HEADS + g + 1, blk(j)] * LOG2E
            if mode == 'dsa':
                s = s + bias
            elif diag:
                s = jnp.where(causal, s, NEG)
            parts = [s[:, c * LANES:(c + 1) * LANES] for c in range(n_sub)]
            top = parts[0]
            for c in range(1, n_sub):
                top = jnp.maximum(top, parts[c])
            top = jnp.broadcast_to(jnp.max(top, axis=1, keepdims=True), (tb, LANES))
            m_old = m_ref[g]
            if mode == 'fox':
                m_new = jnp.maximum(m_old, top + cq_ref[g])
                shift = m_new - cq_ref[g]
            else:
                m_new = jnp.maximum(m_old, top)
                shift = m_new
            m_ref[g] = m_new
            p_ref[g] = jnp.concatenate([jnp.exp2(part - shift).astype(BF16) for part in parts], axis=1)
            acc_ref[g] = (acc_ref[g] + pv) * jnp.exp2(m_old - m_new)

    def full_step(j, carry):
        step(j, False)
        return carry

    lax.fori_loop(0, n_full, full_step, 0)
    step(n_full, True)
    vTs = values(n_full)
    results = []
    for g in range(n_maps):
        acc = acc_ref[g] + _dot_nt(p_ref[g], vTs[g // 2 if mode == 'diff' else g])
        results.append(acc[:, :DH] / acc[:, DH:DH + 1])

    if mode == 'diff':
        lam = _lam_value(lam_ref, lam_init)
        for h in range(N_HEADS):
            o = results[2 * h] - lam * results[2 * h + 1]
            o = _rms(o, gd_ref[:, h * DH:(h + 1) * DH]) * (1.0 - lam_init)
            o_ref[:, h * DH:(h + 1) * DH] = o.astype(o_ref.dtype)
    else:
        for h in range(N_HEADS):
            o_ref[:, h * DH:(h + 1) * DH] = results[h].astype(o_ref.dtype)


def _attn_prompt(mode, q, kT, vT, extra, *, batch, seq, tb, tk, lam_init=0.0):
    nq = seq // tb
    tile = lambda width: pl.BlockSpec((tb, width), lambda b, i: (b * nq + i, 0))
    resident = lambda rows: pl.BlockSpec((None, rows, seq), lambda b, i: (b, 0, 0))
    in_specs = [tile(256), resident(256), resident(256)]
    scratch = []
    if mode == 'fox':
        in_specs += [tile(LANES), resident(16)]
    elif mode == 'diff':
        in_specs += [pl.BlockSpec((4, DC), lambda b, i: (0, 0)), pl.BlockSpec((1, 256), lambda b, i: (0, 0))]
    else:
        in_specs += [tile(256), resident(DI), tile(LANES)]
        scratch = [pltpu.VMEM((seq // tk, tb, tk), I32)]
    n_maps = 2 * N_HEADS if mode == 'diff' else N_HEADS
    if mode == 'fox':
        scratch = [pltpu.VMEM((n_maps, tb, LANES), F32)]
    scratch += [pltpu.VMEM((n_maps, tb, LANES), F32), pltpu.VMEM((n_maps, tb, LANES), F32),
                pltpu.VMEM((n_maps, tb, tk), BF16)]
    return pl.pallas_call(
        functools.partial(_attn_prompt_kernel, mode=mode, tb=tb, tk=tk, k_sel=min(TOPK_MAX, seq // 4),
                          lam_init=lam_init),
        grid=(batch, nq), in_specs=in_specs, out_specs=tile(256),
        out_shape=jax.ShapeDtypeStruct((batch * seq, 256), BF16), scratch_shapes=scratch,
        compiler_params=_cparams(("parallel", "arbitrary")), name=mode + "_prompt",
    )(q, kT, vT, *extra)


def _pad_rows(x, rows):
    return jnp.concatenate([x, jnp.zeros((rows - x.shape[0], x.shape[1]), x.dtype)], axis=0)


def _tile_rows(x, reps):
    return jnp.concatenate([x] * reps, axis=0)


def _prefix_rows(x):
    n = x.shape[0]
    rid = lax.broadcasted_iota(I32, (n, 1), 0)
    out = jnp.zeros_like(x)
    for u in range(n):
        out = out + jnp.where(rid >= u, x[u:u + 1, :], 0.0)
    return out


class _PageStream:
    def __init__(self, pt_ref, caches, bufs, sems, *, layer, group, n_pages, reverse):
        self.pt_ref, self.caches, self.bufs, self.sems = pt_ref, caches, bufs, sems
        self.layer, self.group, self.n_pages, self.reverse = layer, group, n_pages, reverse
        self.n_steps = n_pages // group

    def _copies(self, page, slot, g):
        return [pltpu.make_async_copy(c.at[self.layer, page], buf.at[slot, g], sem.at[slot, g])
                for c, buf, sem in zip(self.caches, self.bufs, self.sems)]

    def start(self, b, step, slot):
        for g in range(self.group):
            p = step * self.group + g
            if self.reverse:
                p = self.n_pages - 1 - p
            for cp in self._copies(self.pt_ref[b, p], slot, g):
                cp.start()

    def wait(self, slot):
        for g in range(self.group):
            for cp in self._copies(0, slot, g):
                cp.wait()

    def run(self, compute):
        b = pl.program_id(0)

        @pl.when(b == 0)
        def _():
            self.start(0, 0, 0)

        def body(s, carry):
            slot = lax.rem(s, 2)

            @pl.when(s + 1 < self.n_steps)
            def _():
                self.start(b, s + 1, 1 - slot)

            @pl.when((s + 1 == self.n_steps) & (b + 1 < pl.num_programs(0)))
            def _():
                self.start(b + 1, 0, 0)

            self.wait(slot)
            compute(s, slot)
            return carry

        lax.fori_loop(0, self.n_steps, body, 0)


def _attn_sample_kernel(pt_ref, *refs, mode, layer, group, n_pages, seq, lam_init):
    q_ref, kn_ref, vn_ref, kc_ref, vc_ref = refs[:5]
    rest = refs[5:]
    if mode == 'fox':
        (lfc_ref, combn_ref, o_ref, qx_ref, m_ref, l_ref, acc_ref, sfx_ref,
         kbuf, vbuf, lfbuf, ksem, vsem, lfsem) = rest
        stream = _PageStream(pt_ref, (kc_ref, vc_ref, lfc_ref), (kbuf, vbuf, lfbuf), (ksem, vsem, lfsem),
                             layer=layer, group=group, n_pages=n_pages, reverse=True)
    else:
        if mode == 'diff':
            lam_ref, gd_ref, o_ref, qx_ref, m_ref, l_ref, acc_ref, kbuf, vbuf, ksem, vsem = rest
        else:
            sel_ref, o_ref, qx_ref, m_ref, l_ref, acc_ref, kbuf, vbuf, ksem, vsem = rest
        stream = _PageStream(pt_ref, (kc_ref, vc_ref), (kbuf, vbuf), (ksem, vsem),
                             layer=layer, group=group, n_pages=n_pages, reverse=False)
    n_maps = 2 * N_HEADS if mode == 'diff' else N_HEADS
    w = 256 // n_maps
    rows = n_maps * seq

    q = q_ref[...]
    for g in range(n_maps):
        qx_ref[g * seq:(g + 1) * seq, :] = jnp.where(_lane_mask(256, g * w, (g + 1) * w), q, 0.0)
    m_ref[...] = jnp.full_like(m_ref, NEG)
    l_ref[...] = jnp.zeros_like(l_ref)
    acc_ref[...] = jnp.zeros_like(acc_ref)
    if mode == 'fox':
        sfx_ref[...] = jnp.zeros_like(sfx_ref)

    def new_prefix():
        cn = _prefix_rows(combn_ref[...])
        return cn, jnp.concatenate([cn[:, h:h + 1] for h in range(N_HEADS)], axis=0)

    def update(s_parts, mask, v_parts, nt):
        s = s_parts[0] if len(s_parts) == 1 else jnp.concatenate(s_parts, axis=1)
        pr, m2, alpha, l2 = _online(s, mask, m_ref[...], l_ref[...])
        m_ref[...] = m2
        l_ref[...] = l2
        pv = None
        for g, v in enumerate(v_parts):
            pg = pr[:, g * LANES:(g + 1) * LANES].astype(BF16)
            t = _dot_nt(pg, v) if nt else _dot(pg, v)
            pv = t if pv is None else pv + t
        acc_ref[...] = acc_ref[...] * alpha + pv

    def past_step(s_idx, slot):
        qx = qx_ref[...].astype(BF16)
        s_parts, v_parts = [], []
        if mode == 'fox':
            r = lax.broadcasted_iota(I32, (LANES, LANES), 0)
            c = lax.broadcasted_iota(I32, (LANES, LANES), 1)
            later = jnp.where(r > c, 1.0, 0.0).astype(BF16)
            _, cq = new_prefix()
            lf_rows = jnp.concatenate(
                [jnp.broadcast_to(lfbuf[slot, g, h:h + 1, :], (seq, LANES))
                 for g in range(group) for h in range(N_HEADS)], axis=0)
            a1, a2, a3 = _split3(lf_rows)
            within = _dot(a1, later) + _dot(a2, later) + _dot(a3, later)
            totals = jnp.sum(lf_rows, axis=1, keepdims=True)
            sfx = sfx_ref[...]
        for g in range(group):
            s = _dot(qx, kbuf[slot, g].astype(BF16))
            if mode == 'fox':
                s = s + (cq + (within[g * rows:(g + 1) * rows, :] + sfx)) * LOG2E
                sfx = sfx + totals[g * rows:(g + 1) * rows, :]
            if mode == 'dsa':
                s = s + _tile_rows(sel_ref[s_idx * group + g], n_maps)
            s_parts.append(s)
            v_parts.append(vbuf[slot, g].astype(BF16))
        if mode == 'fox':
            sfx_ref[...] = sfx
        update(s_parts, None, v_parts, True)

    stream.run(past_step)

    qx = qx_ref[...].astype(BF16)
    kpad = _pad_rows(kn_ref[...], LANES).astype(BF16)
    vpad = _pad_rows(vn_ref[...], LANES).astype(BF16)
    s = _dot_nt(qx, kpad)
    qi = lax.broadcasted_iota(I32, (rows, LANES), 0) % seq
    kj = lax.broadcasted_iota(I32, (rows, LANES), 1)
    mask = kj <= qi
    if mode == 'fox':
        cn, cq = new_prefix()
        cnT = _pad_rows(cn, LANES).T
        ck = jnp.concatenate([jnp.broadcast_to(cnT[h:h + 1, :], (seq, LANES)) for h in range(N_HEADS)], axis=0)
        s = s + (cq - ck) * LOG2E
    if mode == 'dsa':
        s = s + _tile_rows(sel_ref[n_pages], n_maps)
    update([s], mask, [vpad], False)
    res = acc_ref[...] / l_ref[...]
    out = jnp.zeros((seq, 256), F32)
    if mode == 'diff':
        lam = _lam_value(lam_ref, lam_init)
        for h in range(N_HEADS):
            o = res[2 * h * seq:(2 * h + 1) * seq, :] - lam * res[(2 * h + 1) * seq:(2 * h + 2) * seq, :]
            hm = _lane_mask(256, h * DH, (h + 1) * DH)
            ms = jnp.sum(jnp.where(hm, o * o, 0.0), axis=1, keepdims=True) * (1.0 / DH)
            o = o * lax.rsqrt(ms + EPS) * gd_ref[...] * (1.0 - lam_init)
            out = out + jnp.where(hm, o, 0.0)
    else:
        for h in range(N_HEADS):
            out = out + jnp.where(_lane_mask(256, h * DH, (h + 1) * DH), res[h * seq:(h + 1) * seq, :], 0.0)
    o_ref[...] = out.astype(o_ref.dtype)


def _page_scratch(n_caches_rows, group):
    bufs = [pltpu.VMEM((2, group, rows, PAGE_SIZE), F32) for rows in n_caches_rows]
    sems = [pltpu.SemaphoreType.DMA((2, group)) for _ in n_caches_rows]
    return bufs + sems


def _attn_sample(mode, page_table, q, kn, vn, cache_kT, cache_vT, extra, *, layer, batch, seq, group, lam_init=0.0):
    n_pages = page_table.shape[1]
    assert (n_pages // group) % 2 == 0 and n_pages % group == 0
    n_maps = 2 * N_HEADS if mode == 'diff' else N_HEADS
    rows = n_maps * seq
    per_b = lambda width: pl.BlockSpec((None, seq, width), lambda b, pt: (b, 0, 0))
    hbm = pl.BlockSpec(memory_space=pl.ANY)
    in_specs = [per_b(256), per_b(256), per_b(256), hbm, hbm]
    args = [q, kn, vn, cache_kT, cache_vT]
    scratch = [pltpu.VMEM((rows, 256), F32), pltpu.VMEM((rows, 1), F32), pltpu.VMEM((rows, 1), F32),
               pltpu.VMEM((rows, 256), F32)]
    if mode == 'fox':
        cache_lfT, comb_new = extra
        in_specs += [hbm, per_b(LANES)]
        args += [cache_lfT, comb_new]
        scratch += [pltpu.VMEM((rows, 1), F32)] + _page_scratch((256, 256, N_HEADS), group)
    elif mode == 'diff':
        lam4, gd = extra
        in_specs += [pl.BlockSpec((4, DC), lambda b, pt: (0, 0)), pl.BlockSpec((1, 256), lambda b, pt: (0, 0))]
        args += [lam4, gd]
        scratch += _page_scratch((256, 256), group)
    else:
        (sel,) = extra
        in_specs += [pl.BlockSpec((None, n_pages + 1, seq, LANES), lambda b, pt: (b, 0, 0, 0))]
        args += [sel]
        scratch += _page_scratch((256, 256), group)
    return pl.pallas_call(
        functools.partial(_attn_sample_kernel, mode=mode, layer=layer, group=group, n_pages=n_pages, seq=seq,
                          lam_init=lam_init),
        grid_spec=pltpu.PrefetchScalarGridSpec(
            num_scalar_prefetch=1, grid=(batch,), in_specs=in_specs,
            out_specs=pl.BlockSpec((None, seq, 256), lambda b, pt: (b, 0, 0)), scratch_shapes=scratch),
        out_shape=jax.ShapeDtypeStruct((batch, seq, 256), F32),
        compiler_params=_cparams(("arbitrary",)), name=mode + "_sample",
    )(page_table, *args)


def _dsa_select_sample_kernel(pt_ref, qi_ref, kin_ref, combn_ref, ikc_ref, sk_ref, qix_ref, ikbuf, iksem, *,
                              layer, group, n_pages, seq):
    stream = _PageStream(pt_ref, (ikc_ref,), (ikbuf,), (iksem,), layer=layer, group=group, n_pages=n_pages,
                         reverse=False)
    qi = qi_ref[...]
    for h in range(N_HEADS):
        qix_ref[h * seq:(h + 1) * seq, :] = _masked_pair(qi, h, DI)[0]

    def scores(logits):
        comb = combn_ref[...]
        sc = jnp.zeros((seq, LANES), F32)
        for h in range(N_HEADS):
            wcol = comb[:, N_HEADS + h:N_HEADS + h + 1] * (DI ** -0.5 * N_HEADS ** -0.5)
            sc = sc + jnp.maximum(logits[h * seq:(h + 1) * seq, :], 0.0) * wcol
        return sc

    def past_step(s_idx, slot):
        qix = qix_ref[...].astype(BF16)
        for g in range(group):
            ikT = ikbuf[slot, g].astype(BF16)
            sk_ref[s_idx * group + g] = _order_key(scores(_dot(qix, jnp.concatenate([ikT, ikT], axis=0))))

    stream.run(past_step)

    qix = qix_ref[...].astype(BF16)
    kin = kin_ref[...]
    kin2 = _pad_rows(kin + pltpu.roll(kin, DI, axis=1), LANES).astype(BF16)
    sc = scores(_dot_nt(qix, kin2))
    qrow = lax.broadcasted_iota(I32, (seq, LANES), 0)
    kcol = lax.broadcasted_iota(I32, (seq, LANES), 1)
    sk_ref[n_pages] = _order_key(jnp.where(kcol <= qrow, sc, -jnp.inf))


def _dsa_pick_sample_kernel(sk_ref, bias_ref, *, k_sel):
    nb, n_blk, seq, _ = sk_ref.shape

    def count_ge(cand):
        ind = jnp.where(sk_ref[...] >= cand, 1.0, 0.0)
        return jnp.sum(jnp.sum(ind, axis=1, keepdims=True), axis=3, keepdims=True)

    thr = _kth_largest_key(count_ge, float(k_sel), (nb, 1, seq, 1))
    keys = sk_ref[...]
    n_gt = jnp.sum(jnp.sum(jnp.where(keys > thr, 1.0, 0.0), axis=1, keepdims=True), axis=3, keepdims=True)
    need = (float(k_sel) - n_gt)[:, 0]
    eq = keys == thr
    eq2d = jnp.where(eq, 1.0, 0.0).reshape(nb * n_blk * seq, LANES).astype(BF16)
    within = _dot(eq2d, _upper_tri(LANES)).reshape(nb, n_blk, seq, LANES)
    seen = jnp.zeros((nb, seq, 1), F32)
    for j in range(n_blk):
        rank = seen + within[:, j]
        sel = (keys[:, j] > thr[:, 0]) | (eq[:, j] & (rank <= need))
        bias_ref[:, j] = jnp.where(sel, 0.0, NEG)
        seen = rank[:, :, LANES - 1:LANES]


def _dsa_select_sample(page_table, qi, ki_new, comb_new, cache_ikT, *, layer, batch, seq, group):
    n_pages = page_table.shape[1]
    assert (n_pages // group) % 2 == 0 and n_pages % group == 0
    per_b = lambda width: pl.BlockSpec((None, seq, width), lambda b, pt: (b, 0, 0))
    in_specs = [per_b(256), per_b(LANES), per_b(LANES), pl.BlockSpec(memory_space=pl.ANY)]
    k_sel = min(TOPK_MAX, (n_pages * PAGE_SIZE + seq) // 4)
    keys = pl.pallas_call(
        functools.partial(_dsa_select_sample_kernel, layer=layer, group=group, n_pages=n_pages, seq=seq),
        grid_spec=pltpu.PrefetchScalarGridSpec(
            num_scalar_prefetch=1, grid=(batch,), in_specs=in_specs,
            out_specs=pl.BlockSpec((None, n_pages + 1, seq, LANES), lambda b, pt: (b, 0, 0, 0)),
            scratch_shapes=[pltpu.VMEM((N_HEADS * seq, LANES), F32)] + _page_scratch((DI,), group)),
        out_shape=jax.ShapeDtypeStruct((batch, n_pages + 1, seq, LANES), I32),
        compiler_params=_cparams(("arbitrary",)), name="dsa_score_sample",
    )(page_table, qi, ki_new, comb_new, cache_ikT)
    shape = (batch, n_pages + 1, seq, LANES)
    whole = pl.BlockSpec(shape, lambda i: (0, 0, 0, 0))
    return pl.pallas_call(
        functools.partial(_dsa_pick_sample_kernel, k_sel=k_sel), grid=(1,), in_specs=[whole], out_specs=whole,
        out_shape=jax.ShapeDtypeStruct(shape, F32), compiler_params=_cparams(("arbitrary",)), name="dsa_pick_sample",
    )(keys)


def _mix_kernel(x_ref, oa_ref, ob_ref, oc_ref, od_ref, gpre_ref, wg_ref, bg_ref, wpa_ref, wpb_ref, wpc_ref,
                wpd_ref, wo_ref, gpost_ref, y_ref):
    x = x_ref[...]
    h = _rms(x, gpre_ref[...]).astype(BF16)
    mixed = None
    for n, (o_ref, wp_ref) in enumerate(((oa_ref, wpa_ref), (ob_ref, wpb_ref), (oc_ref, wpc_ref), (od_ref, wpd_ref))):
        gate = jax.nn.sigmoid(_dot(h, wg_ref[:, n * D_MODEL:(n + 1) * D_MODEL]) + bg_ref[:, n * D_MODEL:(n + 1) * D_MODEL])
        t = gate * _dot(o_ref[...].astype(BF16), wp_ref[...])
        mixed = t if mixed is None else mixed + t
    y = _dot(mixed.astype(BF16), wo_ref[...])
    y_ref[...] = x + _rms(y, gpost_ref[...])


def _mix(x2d, oa, ob, oc, od, gpre, wg, bg, wpa, wpb, wpc, wpd, wo, gpost, *, tm):
    n = x2d.shape[0]
    tile = lambda width: pl.BlockSpec((tm, width), lambda i: (i, 0))
    const = lambda a: pl.BlockSpec(a.shape, lambda i: (0, 0))
    consts = (gpre, wg, bg, wpa, wpb, wpc, wpd, wo, gpost)
    return pl.pallas_call(
        _mix_kernel, grid=(n // tm,),
        in_specs=[tile(D_MODEL), tile(512), tile(256), tile(256), tile(256)] + [const(a) for a in consts],
        out_specs=tile(D_MODEL), out_shape=jax.ShapeDtypeStruct((n, D_MODEL), F32),
        compiler_params=_cparams(("parallel",)), name="mix",
    )(x2d, oa, ob, oc, od, *consts)


def _ffn_kernel(x_ref, gpre_ref, wgu_ref, wd_ref, gpost_ref, y_ref):
    x = x_ref[...]
    h = _rms(x, gpre_ref[...]).astype(BF16)
    g = _dot(h, wgu_ref[:, :D_FF])
    u = _dot(h, wgu_ref[:, D_FF:])
    a = (g * jax.nn.sigmoid(g) * u).astype(BF16)
    y_ref[...] = x + _rms(_dot(a, wd_ref[...]), gpost_ref[...])


def _ffn(x2d, gpre, wgu, wd, gpost, *, tm):
    n = x2d.shape[0]
    tile = pl.BlockSpec((tm, D_MODEL), lambda i: (i, 0))
    const = lambda a: pl.BlockSpec(a.shape, lambda i: (0, 0))
    consts = (gpre, wgu, wd, gpost)
    return pl.pallas_call(
        _ffn_kernel, grid=(n // tm,), in_specs=[tile] + [const(a) for a in consts],
        out_specs=tile, out_shape=jax.ShapeDtypeStruct((n, D_MODEL), F32),
        compiler_params=_cparams(("parallel",)), name="ffn",
    )(x2d, *consts)


def _rope_tables(pos):
    def one(headw, rot, theta):
        half = rot // 2
        inv = theta ** (-jnp.arange(half, dtype=F32) / half)
        ang = pos.astype(F32)[:, None] * inv[None, :]
        cos, sin = jnp.cos(ang), jnp.sin(ang)
        rest = headw - rot
        n = pos.shape[0]
        c = jnp.concatenate([cos, cos, jnp.ones((n, rest), F32)], axis=1)
        s = jnp.concatenate([-sin, sin, jnp.zeros((n, rest), F32)], axis=1)
        return jnp.tile(c, (1, 256 // headw)), jnp.tile(s, (1, 256 // headw))
    cr, sr = one(DKA, DKA, RET_THETA)
    c32, s32 = one(DC, DC // ROPE_FRAC, ROPE_THETA)
    c64, s64 = one(DH, DH // ROPE_FRAC, ROPE_THETA)
    return (cr, sr, c32, s32, c64, s64)


def _pack_w_in(w_in):
    depth = w_in.shape[0]
    seg = lambda name: w_in[:, :, _SRC[name][0]:_SRC[name][0] + _SRC[name][1]]
    zeros = lambda width: jnp.zeros((depth, D_MODEL, width), w_in.dtype)
    parts = [seg(n) for n in _PACK_ORDER]
    parts += [seg('ki'), zeros(LANES - DI), seg('fb'), seg('wi'), zeros(LANES - 2 * N_HEADS)]
    return jnp.concatenate(parts, axis=2).astype(BF16)


def _heads_T(a, batch, seq):
    return a.reshape(batch, N_HEADS, a.shape[1] // N_HEADS, seq).transpose(0, 3, 1, 2)


def kernel(x_prompt, x_sample, state_ret, cache_fox_k, cache_fox_v, cache_fox_lf, cache_diff_k, cache_diff_v, cache_dsa_k, cache_dsa_v, cache_dsa_ik, page_table, g_pre_mix, g_post_mix, g_pre_ffn, g_post_ffn, w_in, b_forget, lam_q1, lam_k1, lam_q2, lam_k2, g_diff, w_pa, w_pb, w_pc, w_pd, w_gate, b_gate, w_out, w_gu, w_down):
    bp, tp, _ = x_prompt.shape
    bs, ts, _ = x_sample.shape
    depth = w_in.shape[0]
    n_pool = cache_fox_k.shape[1]
    past = page_table.shape[1] * PAGE_SIZE
    tm_p = min(256, tp)
    tb = min(512, tp)
    tk = min(512, tp)
    group = min(16, page_table.shape[1] // 2)

    w_pack = _pack_w_in(w_in)
    bf_pad = jnp.pad(b_forget, ((0, 0), (0, LANES - N_HEADS)))[:, None, :]
    tabs_p = _rope_tables(jnp.arange(tp, dtype=I32))
    tabs_s = _rope_tables(jnp.tile(past + jnp.arange(ts, dtype=I32), bs))
    row = lambda a, l: a[l][None, :]
    bf = lambda a: a.astype(BF16)
    wg16, wpa16, wpb16, wpc16, wpd16, wo16, wgu16, wd16 = map(bf, (w_gate, w_pa, w_pb, w_pc, w_pd, w_out, w_gu, w_down))
    lam4 = jnp.stack([lam_q1, lam_k1, lam_q2, lam_k2], axis=1)
    pageT = lambda c: jnp.moveaxis(c.reshape(depth, n_pool, PAGE_SIZE, -1), 2, 3)
    fkT, fvT, ckT, cvT, dkT, dvT, ikT, lfT = map(pageT, (cache_fox_k, cache_fox_v, cache_diff_k, cache_diff_v,
                                                         cache_dsa_k, cache_dsa_v, cache_dsa_ik, cache_fox_lf))

    xp = x_prompt.reshape(bp * tp, D_MODEL)
    xs = x_sample.reshape(bs * ts, D_MODEL)
    new = {n: [] for n in ('ret_p', 'ret_s', 'fk_p', 'fv_p', 'flf_p', 'fk_s', 'fv_s', 'flf_s', 'ck_p', 'cv_p',
                           'ck_s', 'cv_s', 'dk_p', 'dv_p', 'dik_p', 'dk_s', 'dv_s', 'dik_s')}
    for l in range(depth):
        lam_init = 0.8 - 0.6 * math.exp(-0.3 * l)
        finish = lambda x, oa, ob, oc, od, tm: _ffn(
            _mix(x, oa, ob, oc, od, row(g_pre_mix, l), wg16[l], row(b_gate, l), wpa16[l], wpb16[l], wpc16[l],
                 wpd16[l], wo16[l], row(g_post_mix, l), tm=tm),
            row(g_pre_ffn, l), wgu16[l], wd16[l], row(g_post_ffn, l), tm=tm)
        pr = _proj(xp, row(g_pre_mix, l), w_pack[l], bf_pad[l], tabs_p, mode='prompt', batch=bp, seq=tp, tm=tm_p)
        oa, r_p = _ret_prompt(pr['qa'], pr['ka'], pr['va'], pr['ga'], batch=bp, seq=tp)
        tiles = dict(batch=bp, seq=tp, tb=tb, tk=tk)
        ob = _attn_prompt('fox', pr['qb'], pr['kb'], pr['vb'], (pr['comb'], pr['auxT']), **tiles)
        oc = _attn_prompt('diff', pr['qc'], pr['kc'], pr['vc'], (lam4[l], row(g_diff, l)), lam_init=lam_init, **tiles)
        od = _attn_prompt('dsa', pr['qd'], pr['kd'], pr['vd'], (pr['qi'], pr['ki'], pr['comb']), **tiles)
        xp = finish(xp, oa, ob, oc, od, tm_p)
        new['ret_p'].append(r_p.reshape(bp, N_HEADS, DKA, DVA))
        for name, key in (('fk_p', 'kb'), ('fv_p', 'vb'), ('ck_p', 'kc'), ('cv_p', 'vc'), ('dk_p', 'kd'), ('dv_p', 'vd')):
            new[name].append(_heads_T(pr[key], bp, tp))
        new['flf_p'].append(jnp.swapaxes(pr['auxT'][:, :N_HEADS, :], 1, 2))
        new['dik_p'].append(jnp.swapaxes(pr['ki'], 1, 2))
        sm = _proj(xs, row(g_pre_mix, l), w_pack[l], bf_pad[l], tabs_s, mode='sample', batch=bs, seq=ts, tm=bs * ts)
        oa, r_s = _ret_sample(sm['qa'], sm['ka'], sm['va'], sm['ga'], state_ret[l].reshape(bs, 256, DVA), batch=bs, seq=ts)
        per_b = lambda a: a.reshape(bs, ts, a.shape[-1])
        common = dict(layer=l, batch=bs, seq=ts, group=group)
        ob = _attn_sample('fox', page_table, per_b(sm['qb']), per_b(sm['kb']), per_b(sm['vb']), fkT, fvT,
                          (lfT, per_b(sm['comb'])), **common)
        oc = _attn_sample('diff', page_table, per_b(sm['qc']), per_b(sm['kc']), per_b(sm['vc']), ckT, cvT,
                          (lam4[l], row(g_diff, l)), lam_init=lam_init, **common)
        sel = _dsa_select_sample(page_table, per_b(sm['qi']), per_b(sm['ki']), per_b(sm['comb']), ikT, **common)
        od = _attn_sample('dsa', page_table, per_b(sm['qd']), per_b(sm['kd']), per_b(sm['vd']), dkT, dvT, (sel,), **common)
        flat = lambda a: a.reshape(bs * ts, 256)
        xs = finish(xs, oa, flat(ob), flat(oc), flat(od), bs * ts)
        new['ret_s'].append(r_s.reshape(bs, N_HEADS, DKA, DVA))
        for name, key in (('fk_s', 'kb'), ('fv_s', 'vb'), ('ck_s', 'kc'), ('cv_s', 'vc'), ('dk_s', 'kd'), ('dv_s', 'vd')):
            new[name].append(sm[key].reshape(bs, ts, N_HEADS, DH))
        new['flf_s'].append(sm['comb'][:, :N_HEADS].reshape(bs, ts, N_HEADS))
        new['dik_s'].append(sm['ki'][:, :DI].reshape(bs, ts, DI))
    st = lambda n: jnp.stack(new[n], axis=0)
    return (xp.reshape(bp, tp, D_MODEL), xs.reshape(bs, ts, D_MODEL), st('ret_p'), st('ret_s'),
            st('fk_p'), st('fv_p'), st('flf_p'), st('fk_s'), st('fv_s'), st('flf_s'),
            st('ck_p'), st('cv_p'), st('ck_s'), st('cv_s'),
            st('dk_p'), st('dv_p'), st('dik_p'), st('dk_s'), st('dv_s'), st('dik_s'))
```
